```python
import functools
import jax, jax.numpy as jnp
from jax import lax
import numpy as np


D_MODEL = 1024
BATCH = 2
SEQ = 16384
DEPTH = 2

N_MIXERS = 2
N_SUB = 3
EPS = 1e-6

HG_KDIM = 128
HG_HEADS = D_MODEL // HG_KDIM
HG_VDIM = D_MODEL // HG_HEADS
HG_F = HG_HEADS * HG_KDIM
HG_CHUNK = 64

GM_FFN = 6 * D_MODEL
GM_HALF = GM_FFN // 2
GM_GROUPS = 8
GM_GDIM = GM_HALF // GM_GROUPS
GM_CHUNK = 128

D_FF = ((8 * D_MODEL // 3 + 127) // 128) * 128

N_HGRN = (DEPTH + 1) // 2
N_GMLP = DEPTH // 2

kernel_name = "hybrid_hgrn2_gmlp_macaron_adaln"


def rms_norm(h, g):
    hf = h.astype(jnp.float32)
    y = hf * lax.rsqrt(jnp.mean(hf * hf, axis=-1, keepdims=True) + EPS)
    return (y * g.astype(jnp.float32)).astype(h.dtype)


def layer_norm(h, g, b):
    hf = h.astype(jnp.float32)
    mu = jnp.mean(hf, axis=-1, keepdims=True)
    d = hf - mu
    y = d * lax.rsqrt(jnp.mean(d * d, axis=-1, keepdims=True) + EPS)
    return (y * g.astype(jnp.float32) + b.astype(jnp.float32)).astype(h.dtype)


def swiglu(h, w_in, w_out):
    a, b = jnp.split(h @ w_in, 2, axis=-1)
    return (jax.nn.silu(a) * b) @ w_out


def hgrn2_mix(h, w_in, w_out, out_norm, lb):
    B, S, _ = h.shape
    f32 = jnp.float32
    proj = h @ w_in
    q, f, i, g = jnp.split(proj, [HG_F, 2 * HG_F, 2 * HG_F + D_MODEL], axis=-1)
    q = jax.nn.silu(q.astype(f32))
    fx = f.astype(f32)
    log_f = jnp.logaddexp(jnp.log(lb), jnp.log1p(-lb) + jax.nn.log_sigmoid(fx))
    k = (1.0 - lb) * jax.nn.sigmoid(-fx)
    nc = S // HG_CHUNK

    def to_chunks(t, d):
        return t.reshape(B, nc, HG_CHUNK, HG_HEADS, d).transpose(1, 0, 3, 2, 4)

    qc = to_chunks(q, HG_KDIM)
    kc = to_chunks(k, HG_KDIM)
    vc = to_chunks(i.astype(f32), HG_VDIM)
    bc = jnp.cumsum(to_chunks(log_f, HG_KDIM), axis=3)
    causal = jnp.tril(jnp.ones((HG_CHUNK, HG_CHUNK), bool))[:, :, None]

    def step(state, inp):
        qb, kb, vb, bb = inp
        o_inter = jnp.einsum('bhtk,bhkv->bhtv', qb * jnp.exp(bb), state)
        diff = bb[:, :, :, None, :] - bb[:, :, None, :, :]
        decay = jnp.exp(jnp.where(causal, diff, -jnp.inf))
        scores = jnp.einsum('bhtk,bhtsk,bhsk->bhts', qb, decay, kb)
        o_intra = jnp.einsum('bhts,bhsv->bhtv', scores, vb)
        b_last = bb[:, :, -1:, :]
        k_dec = kb * jnp.exp(b_last - bb)
        new_state = state * jnp.exp(b_last[:, :, 0, :, None]) + jnp.einsum('bhsk,bhsv->bhkv', k_dec, vb)
        return new_state, o_inter + o_intra

    state0 = jnp.zeros((B, HG_HEADS, HG_KDIM, HG_VDIM), f32)
    _, o = lax.scan(step, state0, (qc, kc, vc, bc))
    o = o.transpose(1, 0, 3, 2, 4).reshape(B, S, HG_HEADS, HG_VDIM)
    gate = jax.nn.silu(g.astype(f32)).reshape(B, S, HG_HEADS, HG_VDIM)
    o = rms_norm(o, out_norm) * gate
    return o.reshape(B, S, D_MODEL).astype(h.dtype) @ w_out


def gmlp_mix(h, w_in, b_in, ln_g, ln_b, w_s, b_s, w_out):
    B, S, _ = h.shape
    z = jax.nn.gelu(h @ w_in + b_in)
    u, v = jnp.split(z, 2, axis=-1)
    v = layer_norm(v, ln_g, ln_b)
    nc = S // GM_CHUNK
    vc = v.reshape(B, nc, GM_CHUNK, GM_GROUPS, GM_GDIM)
    ws = w_s * jnp.tril(jnp.ones((GM_CHUNK, GM_CHUNK), w_s.dtype))[None]
    vm = jnp.einsum('gts,bnsgc->bntgc', ws, vc) + b_s.T[None, None, :, :, None]
    return (u * vm.reshape(B, S, GM_HALF)) @ w_out


def sublayer(x, fn, pre_g, post_g, shift, scale, gate, res_w):
    h = rms_norm(x, pre_g) * (1.0 + scale) + shift
    return x + res_w * gate * rms_norm(fn(h), post_g)


def setup_inputs(seed: int = 0) -> dict:
    key = jax.random.key(seed)
    ks = jax.random.split(key, 24)
    f32 = jnp.float32

    def nrm(k, shape, s):
        return s * jax.random.normal(k, shape, f32)

    return {
        "x": nrm(ks[0], (BATCH, SEQ, D_MODEL), 1.0),
        "c": nrm(ks[1], (BATCH, D_MODEL), 1.0),
        "ada_w": nrm(ks[2], (DEPTH, D_MODEL, 3 * N_SUB * D_MODEL), 0.5 * D_MODEL ** -0.5),
        "ada_b": nrm(ks[3], (DEPTH, 3 * N_SUB * D_MODEL), 0.02),
        "norm_pre": 1.0 + nrm(ks[4], (DEPTH, N_SUB, D_MODEL), 0.02),
        "norm_post": 1.0 + nrm(ks[5], (DEPTH, N_SUB, D_MODEL), 0.02),
        "ffn_w_in": nrm(ks[6], (DEPTH, 2, D_MODEL, 2 * D_FF), D_MODEL ** -0.5),
        "ffn_w_out": nrm(ks[7], (DEPTH, 2, D_FF, D_MODEL), D_FF ** -0.5),
        "hg_w_in": nrm(ks[8], (N_HGRN, D_MODEL, 2 * HG_F + 2 * D_MODEL), D_MODEL ** -0.5),
        "hg_w_out": nrm(ks[9], (N_HGRN, D_MODEL, D_MODEL), D_MODEL ** -0.5),
        "hg_out_norm": 1.0 + nrm(ks[10], (N_HGRN, HG_VDIM), 0.02),
        "hg_lb": nrm(ks[11], (DEPTH + 1, HG_F), 0.1),
        "gm_w_in": nrm(ks[12], (N_GMLP, D_MODEL, GM_FFN), D_MODEL ** -0.5),
        "gm_b_in": nrm(ks[13], (N_GMLP, GM_FFN), 0.02),
        "gm_ln_g": 1.0 + nrm(ks[14], (N_GMLP, GM_HALF), 0.02),
        "gm_ln_b": nrm(ks[15], (N_GMLP, GM_HALF), 0.02),
        "gm_w_s": nrm(ks[16], (N_GMLP, GM_GROUPS, GM_CHUNK, GM_CHUNK), GM_CHUNK ** -0.5),
        "gm_b_s": 1.0 + nrm(ks[17], (N_GMLP, GM_GROUPS, GM_CHUNK), 0.02),
        "gm_w_out": nrm(ks[18], (N_GMLP, GM_HALF, D_MODEL), GM_HALF ** -0.5),
    }


def reference(x, c, ada_w, ada_b, norm_pre, norm_post, ffn_w_in, ffn_w_out,
              hg_w_in, hg_w_out, hg_out_norm, hg_lb,
              gm_w_in, gm_b_in, gm_ln_g, gm_ln_b, gm_w_s, gm_b_s, gm_w_out):
    B = x.shape[0]
    lb_all = jnp.cumsum(jax.nn.softmax(hg_lb.astype(jnp.float32), axis=0), axis=0)
    cond = jax.nn.silu(c)
    for i in range(DEPTH):
        mod = (cond @ ada_w[i] + ada_b[i]).reshape(B, 3 * N_SUB, D_MODEL)[:, :, None, :]
        j = i // N_MIXERS
        if i % N_MIXERS == 0:
            mixer = functools.partial(hgrn2_mix, w_in=hg_w_in[j], w_out=hg_w_out[j],
                                      out_norm=hg_out_norm[j], lb=lb_all[i])
        else:
            mixer = functools.partial(gmlp_mix, w_in=gm_w_in[j], b_in=gm_b_in[j], ln_g=gm_ln_g[j],
                                      ln_b=gm_ln_b[j], w_s=gm_w_s[j], b_s=gm_b_s[j], w_out=gm_w_out[j])
        fns = (functools.partial(swiglu, w_in=ffn_w_in[i, 0], w_out=ffn_w_out[i, 0]),
               mixer,
               functools.partial(swiglu, w_in=ffn_w_in[i, 1], w_out=ffn_w_out[i, 1]))
        res_ws = (0.5, 1.0, 0.5)
        for s in range(N_SUB):
            x = sublayer(x, fns[s], norm_pre[i, s], norm_post[i, s],
                         mod[:, 3 * s], mod[:, 3 * s + 1], mod[:, 3 * s + 2], res_ws[s])
    return x
```

```python
import functools

import jax
import jax.numpy as jnp
from jax import lax
from jax.experimental import pallas as pl
from jax.experimental.pallas import tpu as pltpu

EPS = 1e-6
N_SUB = 3
HG_KDIM = 128
HG_CHUNK = 128
HG_BLOCK = 8
GM_CHUNK = 128
GM_GROUPS = 8

V7X_VMEM_LIMIT_BYTES = 56 * 1024 * 1024

F32 = jnp.float32
BF16 = jnp.bfloat16


def _dot(a, b):
    return jnp.dot(a, b, preferred_element_type=F32)


def _dot_nt(a, b):
    return lax.dot_general(a, b, (((1,), (1,)), ((), ())), preferred_element_type=F32)


def _dot_tn(a, b):
    return lax.dot_general(a, b, (((0,), (0,)), ((), ())), preferred_element_type=F32)


def _sigmoid(x):
    return 1.0 / (1.0 + jnp.exp(-x))


def _pre_norm(x, g, scale, shift):
    ms = jnp.mean(x * x, axis=-1, keepdims=True)
    return (x * lax.rsqrt(ms + EPS)) * g * (1.0 + scale) + shift


def _post_residual(x, y, g, gate, res_w):
    ms = jnp.mean(y * y, axis=-1, keepdims=True)
    return x + (res_w * gate) * ((y * lax.rsqrt(ms + EPS)) * g)


def _mod_rows(mod_ref, sub):
    shift = mod_ref[0, 3 * sub:3 * sub + 1, :]
    scale = mod_ref[0, 3 * sub + 1:3 * sub + 2, :]
    gate = mod_ref[0, 3 * sub + 2:3 * sub + 3, :]
    return shift, scale, gate


def _mod_kernel(c_ref, w_ref, b_ref, o_ref):
    c = c_ref[...]
    cond = c * _sigmoid(c)
    o_ref[0] = jnp.dot(cond, w_ref[0], preferred_element_type=F32,
                       precision=lax.Precision.HIGHEST) + b_ref[0]


def _modulation(c, ada_w, ada_b):
    depth, d, n = ada_w.shape
    b = c.shape[0]
    rows = 8
    tn = 1024
    c_pad = jnp.zeros((rows, d), F32).at[:b].set(c)
    out = pl.pallas_call(
        _mod_kernel,
        grid=(depth, n // tn),
        in_specs=[
            pl.BlockSpec((rows, d), lambda i, j: (0, 0)),
            pl.BlockSpec((1, d, tn), lambda i, j: (i, 0, j)),
            pl.BlockSpec((1, 1, tn), lambda i, j: (i, 0, j)),
        ],
        out_specs=pl.BlockSpec((1, rows, tn), lambda i, j: (i, 0, j)),
        out_shape=jax.ShapeDtypeStruct((depth, rows, n), F32),
        compiler_params=pltpu.CompilerParams(
            dimension_semantics=("arbitrary", "arbitrary")),
        name="adaln_mod",
    )(c_pad, ada_w, ada_b.reshape(depth, 1, n))
    return out[:, :b, :].reshape(depth, b, 3 * N_SUB, d)


def _ffn_kernel(x_ref, mod_ref, gpre_ref, gpost_ref, win_ref, wout_ref, o_ref, act_s,
                *, sub, res_w, d_ff, tf):
    x = x_ref[0]
    shift, scale, gate = _mod_rows(mod_ref, sub)
    hb = _pre_norm(x, gpre_ref[...], scale, shift).astype(BF16)
    for j in range(d_ff // tf):
        a = _dot(hb, win_ref[:, j * tf:(j + 1) * tf])
        b = _dot(hb, win_ref[:, d_ff + j * tf:d_ff + (j + 1) * tf])
        act_s[:, j * tf:(j + 1) * tf] = (a * _sigmoid(a) * b).astype(BF16)
    y = _dot(act_s[...], wout_ref[...])
    o_ref[0] = _post_residual(x, y, gpost_ref[...], gate, res_w)


def _const_spec(shape):
    nd = len(shape)
    return pl.BlockSpec(shape, lambda b, i: (0,) * nd, pipeline_mode=pl.Buffered(1))


def _ffn_sublayer(x, mod, gpre, gpost, w_in, w_out, *, sub, res_w, tm=512, tf=256):
    bsz, s, d = x.shape
    d_ff = w_out.shape[0]
    return pl.pallas_call(
        functools.partial(_ffn_kernel, sub=sub, res_w=res_w, d_ff=d_ff, tf=tf),
        grid=(bsz, s // tm),
        in_specs=[
            pl.BlockSpec((1, tm, d), lambda b, i: (b, i, 0)),
            pl.BlockSpec((1, 3 * N_SUB, d), lambda b, i: (b, 0, 0)),
            _const_spec((1, d)),
            _const_spec((1, d)),
            _const_spec((d, 2 * d_ff)),
            _const_spec((d_ff, d)),
        ],
        out_specs=pl.BlockSpec((1, tm, d), lambda b, i: (b, i, 0)),
        out_shape=jax.ShapeDtypeStruct(x.shape, x.dtype),
        scratch_shapes=[pltpu.VMEM((tm, d_ff), BF16)],
        compiler_params=pltpu.CompilerParams(
            dimension_semantics=("arbitrary", "arbitrary"),
            vmem_limit_bytes=V7X_VMEM_LIMIT_BYTES),
        name="ffn_sublayer",
    )(x, mod, gpre.reshape(1, d), gpost.reshape(1, d), w_in, w_out)


def _split3(x):
    hi = x.astype(BF16)
    r1 = x - hi.astype(F32)
    mid = r1.astype(BF16)
    lo = (r1 - mid.astype(F32)).astype(BF16)
    return hi, mid, lo


def _hg_chunk_head(q, k, bc, v, st, pair_level, row_in_block, ones_b):
    c = q.shape[0]
    vb = v.astype(BF16)
    o = _dot_nt((q * jnp.exp(bc)).astype(BF16), st.astype(BF16))
    b_last = bc[c - 1:c, :]
    k_dec = (k * jnp.exp(b_last - bc)).astype(BF16)
    st_new = st * jnp.exp(b_last) + _dot_tn(vb, k_dec)
    scores = jnp.zeros((c, c), F32)
    m = c // 2
    while m >= HG_BLOCK:
        pieces = []
        for g in range(c // (2 * m)):
            r = 2 * m * g + m - 1
            pieces.append(jnp.broadcast_to(bc[r:r + 1, :], (2 * m, bc.shape[1])))
        b_mid = pieces[0] if len(pieces) == 1 else jnp.concatenate(pieces, axis=0)
        w = jnp.exp(-jnp.abs(bc - b_mid))
        sc = _dot_nt((q * w).astype(BF16), (k * w).astype(BF16))
        scores = jnp.where(pair_level == m, sc, scores)
        m //= 2
    o = o + _dot(scores.astype(BF16), vb)
    for d in range(HG_BLOCK):
        if d == 0:
            p = q * k
            v_s = v
        else:
            k_s = pltpu.roll(k, d, 0)
            b_s = pltpu.roll(bc, d, 0)
            v_s = pltpu.roll(v, d, 0)
            p = q * k_s * jnp.exp(jnp.minimum(bc - b_s, 0.0))
            p = jnp.where(row_in_block >= d, p, 0.0)
        o = o + _dot(p.astype(BF16), ones_b) * v_s
    return o, st_new


def _hgrn_kernel(x_ref, mod_ref, gpre_ref, gpost_ref, lb_ref, onorm_ref, win_ref, wout_ref,
                 o_ref, q_s, k_s, b_s, v_s, g_s, o_s, y_s, st_s, *, sub, res_w, layer):
    tm = x_ref.shape[1]
    n_heads, _, kd = q_s.shape
    f = n_heads * kd
    c = HG_CHUNK

    @pl.when(pl.program_id(1) == 0)
    def _():
        st_s[...] = jnp.zeros_like(st_s)

    x = x_ref[0]
    shift, scale, gate = _mod_rows(mod_ref, sub)
    hb = _pre_norm(x, gpre_ref[...], scale, shift).astype(BF16)

    lbr = lb_ref[...]
    e = jnp.exp(lbr - jnp.max(lbr, axis=0, keepdims=True))
    lb = jnp.sum(e[:layer + 1], axis=0, keepdims=True) / jnp.sum(e, axis=0, keepdims=True)

    def store_heads(dst, val):
        for h in range(n_heads):
            dst[h] = val[:, h * kd:(h + 1) * kd]

    qv = _dot(hb, win_ref[:, 0:f])
    store_heads(q_s, qv * _sigmoid(qv))

    fx = _dot(hb, win_ref[:, f:2 * f])
    en = jnp.exp(-jnp.abs(fx))
    inv = 1.0 / (1.0 + en)
    pos = fx >= 0.0
    sig_p = jnp.where(pos, inv, en * inv)
    sig_n = jnp.where(pos, en * inv, inv)
    store_heads(k_s, (1.0 - lb) * sig_n)
    log_f = jnp.log(lb + (1.0 - lb) * sig_p)
    ri = lax.broadcasted_iota(jnp.int32, (c, c), 0)
    ci = lax.broadcasted_iota(jnp.int32, (c, c), 1)
    tri = (ri >= ci).astype(BF16)
    for j in range(tm // c):
        hi, mid, lo = _split3(log_f[j * c:(j + 1) * c, :])
        bc = _dot(tri, hi) + _dot(tri, mid) + _dot(tri, lo)
        for h in range(n_heads):
            b_s[h, j * c:(j + 1) * c, :] = bc[:, h * kd:(h + 1) * kd]

    store_heads(v_s, _dot(hb, win_ref[:, 2 * f:2 * f + f]))
    gv = _dot(hb, win_ref[:, 3 * f:4 * f])
    store_heads(g_s, gv * _sigmoid(gv))

    xr = ri ^ ci
    xr = xr | (xr >> 1)
    xr = xr | (xr >> 2)
    xr = xr | (xr >> 4)
    top = xr - (xr >> 1)
    pair_level = jnp.where((ri > ci) & (top >= HG_BLOCK), top, 0)
    row_in_block = lax.broadcasted_iota(jnp.int32, (c, kd), 0) & (HG_BLOCK - 1)
    ones_b = jnp.ones((kd, kd), BF16)

    def chunk_body(j, carry):
        r0 = pl.multiple_of(j * c, c)
        for h in range(n_heads):
            rows = pl.ds(r0, c)
            o, st_new = _hg_chunk_head(q_s[h, rows, :], k_s[h, rows, :], b_s[h, rows, :],
                                       v_s[h, rows, :], st_s[h], pair_level, row_in_block,
                                       ones_b)
            o_s[h, rows, :] = o
            st_s[h] = st_new
        return carry

    lax.fori_loop(0, tm // c, chunk_body, 0)

    onorm = onorm_ref[...]
    for h in range(n_heads):
        oh = o_s[h]
        ms = jnp.mean(oh * oh, axis=-1, keepdims=True)
        y_s[:, h * kd:(h + 1) * kd] = ((oh * lax.rsqrt(ms + EPS)) * onorm * g_s[h]).astype(BF16)
    y = _dot(y_s[...], wout_ref[...])
    o_ref[0] = _post_residual(x, y, gpost_ref[...], gate, res_w)


def _hgrn_sublayer(x, mod, gpre, gpost, hg_lb, out_norm, w_in, w_out, *, sub, res_w, layer,
                   tm=256):
    bsz, s, d = x.shape
    f = hg_lb.shape[1]
    n_heads = f // HG_KDIM
    head_buf = pltpu.VMEM((n_heads, tm, HG_KDIM), F32)
    return pl.pallas_call(
        functools.partial(_hgrn_kernel, sub=sub, res_w=res_w, layer=layer),
        grid=(bsz, s // tm),
        in_specs=[
            pl.BlockSpec((1, tm, d), lambda b, i: (b, i, 0)),
            pl.BlockSpec((1, 3 * N_SUB, d), lambda b, i: (b, 0, 0)),
            _const_spec((1, d)),
            _const_spec((1, d)),
            _const_spec(hg_lb.shape),
            _const_spec((1, HG_KDIM)),
            _const_spec(w_in.shape),
            _const_spec(w_out.shape),
        ],
        out_specs=pl.BlockSpec((1, tm, d), lambda b, i: (b, i, 0)),
        out_shape=jax.ShapeDtypeStruct(x.shape, x.dtype),
        scratch_shapes=[head_buf, head_buf, head_buf, head_buf, head_buf, head_buf,
                        pltpu.VMEM((tm, d), BF16),
                        pltpu.VMEM((n_heads, HG_KDIM, HG_KDIM), F32)],
        compiler_params=pltpu.CompilerParams(
            dimension_semantics=("arbitrary", "arbitrary"),
            vmem_limit_bytes=V7X_VMEM_LIMIT_BYTES),
        name="hgrn_sublayer",
    )(x, mod, gpre.reshape(1, d), gpost.reshape(1, d), hg_lb, out_norm.reshape(1, HG_KDIM),
      w_in, w_out)


def _gelu_tanh(x):
    return 0.5 * x * (1.0 + jnp.tanh(0.7978845608028654 * (x + 0.044715 * (x * x * x))))


def _gmlp_kernel(x_ref, mod_ref, gpre_ref, gpost_ref, win_ref, bin_ref, lng_ref, lnb_ref,
                 ws_ref, bs_ref, wout_ref, o_ref, vn_s, act_s, *, sub, res_w, tn):
    tm = x_ref.shape[1]
    half = vn_s.shape[1]
    n_groups = ws_ref.shape[0]
    gdim = half // n_groups
    c = GM_CHUNK

    x = x_ref[0]
    shift, scale, gate = _mod_rows(mod_ref, sub)
    hb = _pre_norm(x, gpre_ref[...], scale, shift).astype(BF16)

    vsum = jnp.zeros((tm, 1), F32)
    vs = []
    for j in range(half // tn):
        lo = half + j * tn
        z = _gelu_tanh(_dot(hb, win_ref[:, lo:lo + tn]) + bin_ref[:, lo:lo + tn])
        vsum = vsum + jnp.sum(z, axis=-1, keepdims=True)
        vs.append(z)
    mu = vsum * (1.0 / half)
    vvar = jnp.zeros((tm, 1), F32)
    for z in vs:
        dz = z - mu
        vvar = vvar + jnp.sum(dz * dz, axis=-1, keepdims=True)
    rstd = lax.rsqrt(vvar * (1.0 / half) + EPS)
    for j, z in enumerate(vs):
        sl = slice(j * tn, (j + 1) * tn)
        vn_s[:, sl] = (((z - mu) * rstd) * lng_ref[:, sl] + lnb_ref[:, sl]).astype(BF16)

    ri = lax.broadcasted_iota(jnp.int32, (c, c), 0)
    ci = lax.broadcasted_iota(jnp.int32, (c, c), 1)
    tril = ri >= ci
    for g in range(n_groups):
        wsg = jnp.where(tril, ws_ref[g], 0.0).astype(BF16)
        bsg = bs_ref[g]
        sl = slice(g * gdim, (g + 1) * gdim)
        u = _gelu_tanh(_dot(hb, win_ref[:, sl]) + bin_ref[:, sl])
        for j in range(tm // c):
            rows = slice(j * c, (j + 1) * c)
            vm = _dot(wsg, vn_s[rows, sl]) + bsg
            act_s[rows, sl] = (u[rows, :] * vm).astype(BF16)

    y = _dot(act_s[...], wout_ref[...])
    o_ref[0] = _post_residual(x, y, gpost_ref[...], gate, res_w)


def _gmlp_sublayer(x, mod, gpre, gpost, w_in, b_in, ln_g, ln_b, w_s, b_s, w_out, *, sub, res_w,
                   tm=256, tn=768):
    bsz, s, d = x.shape
    half = w_out.shape[0]
    n_groups = w_s.shape[0]
    return pl.pallas_call(
        functools.partial(_gmlp_kernel, sub=sub, res_w=res_w, tn=tn),
        grid=(bsz, s // tm),
        in_specs=[
            pl.BlockSpec((1, tm, d), lambda b, i: (b, i, 0)),
            pl.BlockSpec((1, 3 * N_SUB, d), lambda b, i: (b, 0, 0)),
            _const_spec((1, d)),
            _const_spec((1, d)),
            _const_spec(w_in.shape),
            _const_spec((1, 2 * half)),
            _const_spec((1, half)),
            _const_spec((1, half)),
            _const_spec(w_s.shape),
            _const_spec((n_groups, GM_CHUNK, 1)),
            _const_spec(w_out.shape),
        ],
        out_specs=pl.BlockSpec((1, tm, d), lambda b, i: (b, i, 0)),
        out_shape=jax.ShapeDtypeStruct(x.shape, x.dtype),
        scratch_shapes=[pltpu.VMEM((tm, half), BF16), pltpu.VMEM((tm, half), BF16)],
        compiler_params=pltpu.CompilerParams(
            dimension_semantics=("arbitrary", "arbitrary"),
            vmem_limit_bytes=V7X_VMEM_LIMIT_BYTES),
        name="gmlp_sublayer",
    )(x, mod, gpre.reshape(1, d), gpost.reshape(1, d), w_in, b_in.reshape(1, 2 * half),
      ln_g.reshape(1, half), ln_b.reshape(1, half), w_s, b_s.reshape(n_groups, GM_CHUNK, 1),
      w_out)


def kernel(x, c, ada_w, ada_b, norm_pre, norm_post, ffn_w_in, ffn_w_out, hg_w_in, hg_w_out,
           hg_out_norm, hg_lb, gm_w_in, gm_b_in, gm_ln_g, gm_ln_b, gm_w_s, gm_b_s, gm_w_out):
    depth = ada_w.shape[0]
    mod = _modulation(c, ada_w, ada_b)
    ffn_w_in = ffn_w_in.astype(BF16)
    ffn_w_out = ffn_w_out.astype(BF16)
    hg_w_in = hg_w_in.astype(BF16)
    hg_w_out = hg_w_out.astype(BF16)
    gm_w_in = gm_w_in.astype(BF16)
    gm_w_out = gm_w_out.astype(BF16)
    for i in range(depth):
        j = i // 2
        x = _ffn_sublayer(x, mod[i], norm_pre[i, 0], norm_post[i, 0], ffn_w_in[i, 0],
                          ffn_w_out[i, 0], sub=0, res_w=0.5)
        if i % 2 == 0:
            x = _hgrn_sublayer(x, mod[i], norm_pre[i, 1], norm_post[i, 1], hg_lb,
                               hg_out_norm[j], hg_w_in[j], hg_w_out[j], sub=1, res_w=1.0,
                               layer=i)
        else:
            x = _gmlp_sublayer(x, mod[i], norm_pre[i, 1], norm_post[i, 1], gm_w_in[j],
                               gm_b_in[j], gm_ln_g[j], gm_ln_b[j], gm_w_s[j], gm_b_s[j],
                               gm_w_out[j], sub=1, res_w=1.0)
        x = _ffn_sublayer(x, mod[i], norm_pre[i, 2], norm_post[i, 2], ffn_w_in[i, 1],
                          ffn_w_out[i, 1], sub=2, res_w=0.5)
    return x
```

```python
import functools

import jax
import jax.numpy as jnp
from jax import lax
from jax.experimental import pallas as pl
from jax.experimental.pallas import tpu as pltpu

EPS = 1e-6
LOG2_E = 1.4426950408889634
N_SUB = 3
HG_KDIM = 128
HG_CHUNK = 128
HG_BLOCK = 8
GM_CHUNK = 128
GM_GROUPS = 8

V7X_VMEM_LIMIT_BYTES = 56 * 1024 * 1024

F32 = jnp.float32
BF16 = jnp.bfloat16


def _dot(a, b):
    return jnp.dot(a, b, preferred_element_type=F32)


def _dot_nt(a, b):
    return lax.dot_general(a, b, (((1,), (1,)), ((), ())), preferred_element_type=F32)


def _dot_tn(a, b):
    return lax.dot_general(a, b, (((0,), (0,)), ((), ())), preferred_element_type=F32)


def _sigmoid(x):
    return 1.0 / (1.0 + jnp.exp(-x))


def _pre_norm(x, g, scale, shift):
    ms = jnp.mean(x * x, axis=-1, keepdims=True)
    return (x * lax.rsqrt(ms + EPS)) * g * (1.0 + scale) + shift


def _post_residual(x, y, g, gate, res_w):
    ms = jnp.mean(y * y, axis=-1, keepdims=True)
    return x + (res_w * gate) * ((y * lax.rsqrt(ms + EPS)) * g)


def _mod_rows(mod_ref, sub):
    shift = mod_ref[0, 3 * sub:3 * sub + 1, :]
    scale = mod_ref[0, 3 * sub + 1:3 * sub + 2, :]
    gate = mod_ref[0, 3 * sub + 2:3 * sub + 3, :]
    return shift, scale, gate


def _mod_kernel(c_ref, w_ref, b_ref, o_ref):
    c = c_ref[...]
    cond = c * _sigmoid(c)
    o_ref[0] = jnp.dot(cond, w_ref[0], preferred_element_type=F32,
                       precision=lax.Precision.HIGHEST) + b_ref[0]


def _modulation(c, ada_w, ada_b):
    depth, d, n = ada_w.shape
    b = c.shape[0]
    rows = 8
    tn = 1024
    c_pad = jnp.zeros((rows, d), F32).at[:b].set(c)
    out = pl.pallas_call(
        _mod_kernel,
        grid=(depth, n // tn),
        in_specs=[
            pl.BlockSpec((rows, d), lambda i, j: (0, 0)),
            pl.BlockSpec((1, d, tn), lambda i, j: (i, 0, j)),
            pl.BlockSpec((1, 1, tn), lambda i, j: (i, 0, j)),
        ],
        out_specs=pl.BlockSpec((1, rows, tn), lambda i, j: (i, 0, j)),
        out_shape=jax.ShapeDtypeStruct((depth, rows, n), F32),
        compiler_params=pltpu.CompilerParams(
            dimension_semantics=("arbitrary", "arbitrary")),
        name="adaln_mod",
    )(c_pad, ada_w, ada_b.reshape(depth, 1, n))
    return out[:, :b, :].reshape(depth, b, 3 * N_SUB, d)


def _ffn_kernel(x_ref, mod_ref, gpre_ref, gpost_ref, win_ref, wout_ref, o_ref, act_s, hb_s,
                *, sub, res_w, d_ff, tf):
    x = x_ref[0]
    shift, scale, gate = _mod_rows(mod_ref, sub)
    hb_s[...] = _pre_norm(x, gpre_ref[...], scale, shift).astype(BF16)

    def activate(j, a, b):
        act_s[:, j * tf:(j + 1) * tf] = (a * _sigmoid(a) * b).astype(BF16)

    pending = None
    for j in range(d_ff // tf):
        a = _dot(hb_s[...], win_ref[:, j * tf:(j + 1) * tf])
        b = _dot(hb_s[...], win_ref[:, d_ff + j * tf:d_ff + (j + 1) * tf])
        if pending is not None:
            pending()
        pending = functools.partial(activate, j, a, b)
    pending()
    y = _dot(act_s[...], wout_ref[...])
    o_ref[0] = _post_residual(x_ref[0], y, gpost_ref[...], gate, res_w)


def _const_spec(shape):
    nd = len(shape)
    return pl.BlockSpec(shape, lambda b, i: (0,) * nd, pipeline_mode=pl.Buffered(1))


def _ffn_sublayer(x, mod, gpre, gpost, w_in, w_out, *, sub, res_w, tm=512, tf=256):
    bsz, s, d = x.shape
    d_ff = w_out.shape[0]
    return pl.pallas_call(
        functools.partial(_ffn_kernel, sub=sub, res_w=res_w, d_ff=d_ff, tf=tf),
        grid=(bsz, s // tm),
        in_specs=[
            pl.BlockSpec((1, tm, d), lambda b, i: (b, i, 0)),
            pl.BlockSpec((1, 3 * N_SUB, d), lambda b, i: (b, 0, 0)),
            _const_spec((1, d)),
            _const_spec((1, d)),
            _const_spec((d, 2 * d_ff)),
            _const_spec((d_ff, d)),
        ],
        out_specs=pl.BlockSpec((1, tm, d), lambda b, i: (b, i, 0)),
        out_shape=jax.ShapeDtypeStruct(x.shape, x.dtype),
        scratch_shapes=[pltpu.VMEM((tm, d_ff), BF16), pltpu.VMEM((tm, d), BF16)],
        compiler_params=pltpu.CompilerParams(
            dimension_semantics=("arbitrary", "arbitrary"),
            vmem_limit_bytes=V7X_VMEM_LIMIT_BYTES),
        name="ffn_sublayer",
    )(x, mod, gpre.reshape(1, d), gpost.reshape(1, d), w_in, w_out)


def _split3(x):
    hi = x.astype(BF16)
    r1 = x - hi.astype(F32)
    mid = r1.astype(BF16)
    lo = (r1 - mid.astype(F32)).astype(BF16)
    return hi, mid, lo


def _rows_bcast(ref, h, rows, reps):
    pieces = [jnp.broadcast_to(ref[h, pl.ds(r, 1), :], (reps, ref.shape[2])) for r in rows]
    return pieces[0] if len(pieces) == 1 else jnp.concatenate(pieces, axis=0)


def _hg_chunk_heads(q_s, k_s, b_s, v_s, heads, r0, sts, pair_code, sel_ref):
    c = HG_CHUNK
    rows = pl.ds(r0, c)
    hs = range(len(heads))
    q = [q_s[h, rows, :] for h in heads]
    k = [k_s[h, rows, :] for h in heads]
    bc = [b_s[h, rows, :] for h in heads]
    vb = [v_s[h, rows, :] for h in heads]
    ps = [[] for _ in hs]
    for j in range(HG_BLOCK):
        src = [r0 + HG_BLOCK * n + j for n in range(c // HG_BLOCK)]
        for i in hs:
            k_j = _rows_bcast(k_s, heads[i], src, HG_BLOCK)
            b_j = _rows_bcast(b_s, heads[i], src, HG_BLOCK)
            ps[i].append((q[i] * k_j * jnp.exp2(jnp.minimum(bc[i] - b_j, 0.0))).astype(BF16))
    scores = [_dot(jnp.concatenate(ps[i], axis=1), sel_ref[...]) for i in hs]
    m = HG_BLOCK
    while m < c:
        mids = [r0 + 2 * m * g + m - 1 for g in range(c // (2 * m))]
        for i in hs:
            w = jnp.exp2(-jnp.abs(bc[i] - _rows_bcast(b_s, heads[i], mids, 2 * m)))
            sc = _dot_nt((q[i] * w).astype(BF16), (k[i] * w).astype(BF16))
            scores[i] = jnp.where(pair_code == m, sc, scores[i])
        m *= 2
    o = [_dot_nt((q[i] * jnp.exp2(bc[i])).astype(BF16), sts[i].astype(BF16)) for i in hs]
    st_new = []
    for i in hs:
        b_last = b_s[heads[i], pl.ds(r0 + c - 1, 1), :]
        k_dec = (k[i] * jnp.exp2(b_last - bc[i])).astype(BF16)
        st_new.append(sts[i] * jnp.exp2(b_last) + _dot_tn(vb[i], k_dec))
    for i in hs:
        a = jnp.where(pair_code == 0, 0.0, scores[i]).astype(BF16)
        o[i] = o[i] + _dot(a, vb[i])
    return o, st_new


def _hgrn_kernel(x_ref, mod_ref, gpre_ref, gpost_ref, lb_ref, onorm_ref, win_ref, wout_ref,
                 o_ref, q_s, k_s, b_s, g_s, v_s, o_s, y_s, hb_s, st_s, sel_ref,
                 *, sub, res_w, layer, lockstep, proj_heads):
    tm = x_ref.shape[1]
    n_heads, _, kd = q_s.shape
    f = n_heads * kd
    c = HG_CHUNK

    @pl.when(pl.program_id(1) == 0)
    def _():
        st_s[...] = jnp.zeros_like(st_s)

    x = x_ref[0]
    shift, scale, gate = _mod_rows(mod_ref, sub)
    hb_s[...] = _pre_norm(x, gpre_ref[...], scale, shift).astype(BF16)

    lbr = lb_ref[...]
    e = jnp.exp(lbr - jnp.max(lbr, axis=0, keepdims=True))
    lb = jnp.sum(e[:layer + 1], axis=0, keepdims=True) / jnp.sum(e, axis=0, keepdims=True)

    ri = lax.broadcasted_iota(jnp.int32, (c, c), 0)
    ci = lax.broadcasted_iota(jnp.int32, (c, c), 1)
    tri = (ri >= ci).astype(BF16)

    def project(kind, p):
        lo = kind * f + p * proj_heads * kd
        return _dot(hb_s[...], win_ref[:, lo:lo + proj_heads * kd])

    def store_heads(dst, p, val, rows=slice(None)):
        for i in range(proj_heads):
            dst[p * proj_heads + i, rows, :] = val[:, i * kd:(i + 1) * kd]

    def q_piece(p, qv):
        store_heads(q_s, p, qv * _sigmoid(qv))

    def f_piece(p, fx):
        lbp = lb[:, p * proj_heads * kd:(p + 1) * proj_heads * kd]
        en = jnp.exp(-jnp.abs(fx))
        inv = 1.0 / (1.0 + en)
        pos = fx >= 0.0
        sig_p = jnp.where(pos, inv, en * inv)
        sig_n = jnp.where(pos, en * inv, inv)
        store_heads(k_s, p, (1.0 - lbp) * sig_n)
        log2_f = jnp.log(lbp + (1.0 - lbp) * sig_p) * LOG2_E
        for j in range(tm // c):
            hi, mid, lo = _split3(log2_f[j * c:(j + 1) * c, :])
            bc = _dot(tri, hi) + _dot(tri, mid) + _dot(tri, lo)
            store_heads(b_s, p, bc, slice(j * c, (j + 1) * c))

    def v_piece(p, vv):
        store_heads(v_s, p, vv.astype(BF16))

    def g_piece(p, gv):
        store_heads(g_s, p, gv * _sigmoid(gv))

    work = [(piece, kind, p) for p in range(n_heads // proj_heads)
            for kind, piece in ((1, f_piece), (0, q_piece), (3, g_piece), (2, v_piece))]
    pending = None
    for piece, kind, p in work:
        res = project(kind, p)
        if pending is not None:
            pending()
        pending = functools.partial(piece, p, res)
    pending()

    xr = ri ^ ci
    xr = xr | (xr >> 1)
    xr = xr | (xr >> 2)
    xr = xr | (xr >> 4)
    top = xr - (xr >> 1)
    pair_code = jnp.where(ri >= ci, jnp.where(top >= HG_BLOCK, top, 1), 0)
    sel_j = lax.broadcasted_iota(jnp.int32, (HG_BLOCK * kd, c), 0) // kd
    sel_s = lax.broadcasted_iota(jnp.int32, (HG_BLOCK * kd, c), 1) & (HG_BLOCK - 1)
    sel_ref[...] = (sel_j == sel_s).astype(BF16)

    def chunk_body(j, carry):
        r0 = pl.multiple_of(j * c, c)
        for h0 in range(0, n_heads, lockstep):
            heads = list(range(h0, h0 + lockstep))
            o, st_new = _hg_chunk_heads(q_s, k_s, b_s, v_s, heads, r0, [st_s[h] for h in heads],
                                        pair_code, sel_ref)
            for i, h in enumerate(heads):
                o_s[h, pl.ds(r0, c), :] = o[i]
                st_s[h] = st_new[i]
        return carry

    lax.fori_loop(0, tm // c, chunk_body, 0)

    onorm = onorm_ref[...]
    for h in range(n_heads):
        oh = o_s[h]
        ms = jnp.mean(oh * oh, axis=-1, keepdims=True)
        y_s[:, h * kd:(h + 1) * kd] = ((oh * lax.rsqrt(ms + EPS)) * onorm * g_s[h]).astype(BF16)
    y = _dot(y_s[...], wout_ref[...])
    o_ref[0] = _post_residual(x_ref[0], y, gpost_ref[...], gate, res_w)


def _hgrn_sublayer(x, mod, gpre, gpost, hg_lb, out_norm, w_in, w_out, *, sub, res_w, layer,
                   tm=256, lockstep=4, proj_heads=8):
    bsz, s, d = x.shape
    f = hg_lb.shape[1]
    n_heads = f // HG_KDIM
    head_buf = pltpu.VMEM((n_heads, tm, HG_KDIM), F32)
    return pl.pallas_call(
        functools.partial(_hgrn_kernel, sub=sub, res_w=res_w, layer=layer, lockstep=lockstep,
                          proj_heads=proj_heads),
        grid=(bsz, s // tm),
        in_specs=[
            pl.BlockSpec((1, tm, d), lambda b, i: (b, i, 0)),
            pl.BlockSpec((1, 3 * N_SUB, d), lambda b, i: (b, 0, 0)),
            _const_spec((1, d)),
            _const_spec((1, d)),
            _const_spec(hg_lb.shape),
            _const_spec((1, HG_KDIM)),
            _const_spec(w_in.shape),
            _const_spec(w_out.shape),
        ],
        out_specs=pl.BlockSpec((1, tm, d), lambda b, i: (b, i, 0)),
        out_shape=jax.ShapeDtypeStruct(x.shape, x.dtype),
        scratch_shapes=[head_buf, head_buf, head_buf, head_buf,
                        pltpu.VMEM((n_heads, tm, HG_KDIM), BF16),
                        head_buf,
                        pltpu.VMEM((tm, d), BF16),
                        pltpu.VMEM((tm, d), BF16),
                        pltpu.VMEM((n_heads, HG_KDIM, HG_KDIM), F32),
                        pltpu.VMEM((HG_BLOCK * HG_KDIM, HG_CHUNK), BF16)],
        compiler_params=pltpu.CompilerParams(
            dimension_semantics=("arbitrary", "arbitrary"),
            vmem_limit_bytes=V7X_VMEM_LIMIT_BYTES),
        name="hgrn_sublayer",
    )(x, mod, gpre.reshape(1, d), gpost.reshape(1, d), hg_lb, out_norm.reshape(1, HG_KDIM),
      w_in, w_out)


def _gelu_tanh(x):
    return 0.5 * x * (1.0 + jnp.tanh(0.7978845608028654 * (x + 0.044715 * (x * x * x))))


def _gmlp_kernel(x_ref, mod_ref, gpre_ref, gpost_ref, win_ref, bin_ref, lng_ref, lnb_ref,
                 ws_ref, bs_ref, wout_ref, o_ref, z_s, vn_s, act_s, hb_s, *, sub, res_w, tn):
    tm = x_ref.shape[1]
    half = vn_s.shape[1]
    n_groups = ws_ref.shape[0]
    gdim = half // n_groups
    n_tiles = half // tn
    c = GM_CHUNK

    shift, scale, gate = _mod_rows(mod_ref, sub)
    hb_s[...] = _pre_norm(x_ref[0], gpre_ref[...], scale, shift).astype(BF16)

    def project(lo):
        return _dot(hb_s[...], win_ref[:, lo:lo + tn])

    sums, sqs = [], []

    def v_act(j, res):
        z = _gelu_tanh(res + bin_ref[:, half + j * tn:half + (j + 1) * tn])
        z_s[:, j * tn:(j + 1) * tn] = z
        sums.append(jnp.sum(z, axis=-1, keepdims=True))
        sqs.append(jnp.sum(z * z, axis=-1, keepdims=True))

    def normalize():
        mu = sum(sums) * (1.0 / half)
        var = sum(sqs) * (1.0 / half) - mu * mu
        rstd = lax.rsqrt(var + EPS)
        for j in range(n_tiles):
            sl = slice(j * tn, (j + 1) * tn)
            vn_s[:, sl] = (((z_s[:, sl] - mu) * rstd) * lng_ref[:, sl] + lnb_ref[:, sl]).astype(BF16)

    ri = lax.broadcasted_iota(jnp.int32, (c, c), 0)
    ci = lax.broadcasted_iota(jnp.int32, (c, c), 1)
    tril = ri >= ci

    def u_act(j, res):
        u = _gelu_tanh(res + bin_ref[:, j * tn:(j + 1) * tn])
        for g in range(j * tn // gdim, (j + 1) * tn // gdim):
            wsg = jnp.where(tril, ws_ref[g], 0.0).astype(BF16)
            bsg = bs_ref[g]
            sl = slice(g * gdim, (g + 1) * gdim)
            for i in range(tm // c):
                rows = slice(i * c, (i + 1) * c)
                vm = _dot(wsg, vn_s[rows, sl]) + bsg
                act_s[rows, sl] = (u[rows, g * gdim - j * tn:(g + 1) * gdim - j * tn] * vm).astype(BF16)

    work = [(half + j * tn, functools.partial(v_act, j)) for j in range(n_tiles)]
    work += [(j * tn, functools.partial(u_act, j)) for j in range(n_tiles)]
    pending = []
    for idx, (lo, act) in enumerate(work):
        res = project(lo)
        for fn in pending:
            fn()
        pending = [functools.partial(act, res)]
        if idx == n_tiles:
            pending.insert(0, normalize)
    for fn in pending:
        fn()

    y = _dot(act_s[...], wout_ref[...])
    o_ref[0] = _post_residual(x_ref[0], y, gpost_ref[...], gate, res_w)


def _gmlp_sublayer(x, mod, gpre, gpost, w_in, b_in, ln_g, ln_b, w_s, b_s, w_out, *, sub, res_w,
                   tm=256, tn=768):
    bsz, s, d = x.shape
    half = w_out.shape[0]
    n_groups = w_s.shape[0]
    return pl.pallas_call(
        functools.partial(_gmlp_kernel, sub=sub, res_w=res_w, tn=tn),
        grid=(bsz, s // tm),
        in_specs=[
            pl.BlockSpec((1, tm, d), lambda b, i: (b, i, 0)),
            pl.BlockSpec((1, 3 * N_SUB, d), lambda b, i: (b, 0, 0)),
            _const_spec((1, d)),
            _const_spec((1, d)),
            _const_spec(w_in.shape),
            _const_spec((1, 2 * half)),
            _const_spec((1, half)),
            _const_spec((1, half)),
            _const_spec(w_s.shape),
            _const_spec((n_groups, GM_CHUNK, 1)),
            _const_spec(w_out.shape),
        ],
        out_specs=pl.BlockSpec((1, tm, d), lambda b, i: (b, i, 0)),
        out_shape=jax.ShapeDtypeStruct(x.shape, x.dtype),
        scratch_shapes=[pltpu.VMEM((tm, half), F32), pltpu.VMEM((tm, half), BF16),
                        pltpu.VMEM((tm, half), BF16), pltpu.VMEM((tm, d), BF16)],
        compiler_params=pltpu.CompilerParams(
            dimension_semantics=("arbitrary", "arbitrary"),
            vmem_limit_bytes=V7X_VMEM_LIMIT_BYTES),
        name="gmlp_sublayer",
    )(x, mod, gpre.reshape(1, d), gpost.reshape(1, d), w_in, b_in.reshape(1, 2 * half),
      ln_g.reshape(1, half), ln_b.reshape(1, half), w_s, b_s.reshape(n_groups, GM_CHUNK, 1),
      w_out)


def kernel(x, c, ada_w, ada_b, norm_pre, norm_post, ffn_w_in, ffn_w_out, hg_w_in, hg_w_out,
           hg_out_norm, hg_lb, gm_w_in, gm_b_in, gm_ln_g, gm_ln_b, gm_w_s, gm_b_s, gm_w_out):
    depth = ada_w.shape[0]
    mod = _modulation(c, ada_w, ada_b)
    ffn_w_in = ffn_w_in.astype(BF16)
    ffn_w_out = ffn_w_out.astype(BF16)
    hg_w_in = hg_w_in.astype(BF16)
    hg_w_out = hg_w_out.astype(BF16)
    gm_w_in = gm_w_in.astype(BF16)
    gm_w_out = gm_w_out.astype(BF16)
    for i in range(depth):
        j = i // 2
        x = _ffn_sublayer(x, mod[i], norm_pre[i, 0], norm_post[i, 0], ffn_w_in[i, 0],
                          ffn_w_out[i, 0], sub=0, res_w=0.5)
        if i % 2 == 0:
            x = _hgrn_sublayer(x, mod[i], norm_pre[i, 1], norm_post[i, 1], hg_lb,
                               hg_out_norm[j], hg_w_in[j], hg_w_out[j], sub=1, res_w=1.0,
                               layer=i)
        else:
            x = _gmlp_sublayer(x, mod[i], norm_pre[i, 1], norm_post[i, 1], gm_w_in[j],
                               gm_b_in[j], gm_ln_g[j], gm_ln_b[j], gm_w_s[j], gm_b_s[j],
                               gm_w_out[j], sub=1, res_w=1.0)
        x = _ffn_sublayer(x, mod[i], norm_pre[i, 2], norm_post[i, 2], ffn_w_in[i, 1],
                          ffn_w_out[i, 1], sub=2, res_w=0.5)
    return x
```

```python
import functools

import jax
import jax.numpy as jnp
from jax import lax
from jax.experimental import pallas as pl
from jax.experimental.pallas import tpu as pltpu

EPS = 1e-6
LOG2_E = 1.4426950408889634
N_SUB = 3
HG_KDIM = 128
HG_CHUNK = 128
HG_BLOCK = 8
GM_CHUNK = 128
GM_GROUPS = 8

V7X_VMEM_LIMIT_BYTES = 56 * 1024 * 1024

F32 = jnp.float32
BF16 = jnp.bfloat16


def _dot(a, b):
    return jnp.dot(a, b, preferred_element_type=F32)


def _dot_nt(a, b):
    return lax.dot_general(a, b, (((1,), (1,)), ((), ())), preferred_element_type=F32)


def _dot_tn(a, b):
    return lax.dot_general(a, b, (((0,), (0,)), ((), ())), preferred_element_type=F32)


def _sigmoid(x):
    return 1.0 / (1.0 + jnp.exp(-x))


def _pre_norm(x, g, scale, shift):
    ms = jnp.mean(x * x, axis=-1, keepdims=True)
    return (x * lax.rsqrt(ms + EPS)) * g * (1.0 + scale) + shift


def _post_residual(x, y, g, gate, res_w):
    ms = jnp.mean(y * y, axis=-1, keepdims=True)
    return x + (res_w * gate) * ((y * lax.rsqrt(ms + EPS)) * g)


def _mod_rows(mod_ref, sub):
    shift = mod_ref[0, 3 * sub:3 * sub + 1, :]
    scale = mod_ref[0, 3 * sub + 1:3 * sub + 2, :]
    gate = mod_ref[0, 3 * sub + 2:3 * sub + 3, :]
    return shift, scale, gate


def _mod_kernel(c_ref, w_ref, b_ref, o_ref):
    c = c_ref[...]
    cond = c * _sigmoid(c)
    o_ref[0] = jnp.dot(cond, w_ref[0], preferred_element_type=F32,
                       precision=lax.Precision.HIGHEST) + b_ref[0]


def _modulation(c, ada_w, ada_b):
    depth, d, n = ada_w.shape
    b = c.shape[0]
    rows = 8
    tn = 1024
    c_pad = jnp.zeros((rows, d), F32).at[:b].set(c)
    out = pl.pallas_call(
        _mod_kernel,
        grid=(depth, n // tn),
        in_specs=[
            pl.BlockSpec((rows, d), lambda i, j: (0, 0)),
            pl.BlockSpec((1, d, tn), lambda i, j: (i, 0, j)),
            pl.BlockSpec((1, 1, tn), lambda i, j: (i, 0, j)),
        ],
        out_specs=pl.BlockSpec((1, rows, tn), lambda i, j: (i, 0, j)),
        out_shape=jax.ShapeDtypeStruct((depth, rows, n), F32),
        compiler_params=pltpu.CompilerParams(
            dimension_semantics=("arbitrary", "arbitrary")),
        name="adaln_mod",
    )(c_pad, ada_w, ada_b.reshape(depth, 1, n))
    return out.reshape(depth, rows, 3 * N_SUB, d)


def _emit_skewed(streams, skew):
    n = max(len(s) + i * skew for i, s in enumerate(streams))
    for k in range(n):
        for i, s in enumerate(streams):
            if 0 <= k - i * skew < len(s):
                s[k - i * skew]()


def _ffn_kernel(x_ref, mod_ref, gpre_ref, gpost_ref, win_ref, wout_ref, o_ref, act_s, hb_s,
                *, sub, res_w, d_ff, tf, n_sub, skew):
    tm = x_ref.shape[1]
    shift, scale, gate = _mod_rows(mod_ref, sub)

    def row_steps(rows):
        res = {}

        def pre():
            hb_s[rows, :] = _pre_norm(x_ref[0, rows, :], gpre_ref[...], scale, shift).astype(BF16)

        def up(j):
            if j < d_ff // tf:
                hb = hb_s[rows, :]
                res[j] = (_dot(hb, win_ref[:, j * tf:(j + 1) * tf]),
                          _dot(hb, win_ref[:, d_ff + j * tf:d_ff + (j + 1) * tf]))
            if j > 0:
                a, b = res.pop(j - 1)
                act_s[rows, (j - 1) * tf:j * tf] = (a * _sigmoid(a) * b).astype(BF16)

        def down():
            res["y"] = _dot(act_s[rows, :], wout_ref[...])

        def post():
            o_ref[0, rows, :] = _post_residual(x_ref[0, rows, :], res.pop("y"), gpost_ref[...],
                                               gate, res_w)

        return [pre] + [functools.partial(up, j) for j in range(d_ff // tf + 1)] + [down, post]

    rt = tm // n_sub
    _emit_skewed([row_steps(slice(i * rt, (i + 1) * rt)) for i in range(n_sub)], skew)


def _sublayer_call(body, name, x, mod, layer, params, scratch_shapes, tm):
    bsz, s, d = x.shape

    def picked(arr, idx):
        rest = (0,) * (arr.ndim - len(idx))
        return pl.BlockSpec((None,) * len(idx) + arr.shape[len(idx):],
                            lambda b, i: idx + rest, pipeline_mode=pl.Buffered(1))

    return pl.pallas_call(
        body,
        grid=(bsz, s // tm),
        in_specs=[pl.BlockSpec((1, tm, d), lambda b, i: (b, i, 0)),
                  pl.BlockSpec((None, 1, 3 * N_SUB, d), lambda b, i: (layer, b, 0, 0))]
        + [picked(arr, idx) for arr, idx in params],
        out_specs=pl.BlockSpec((1, tm, d), lambda b, i: (b, i, 0)),
        out_shape=jax.ShapeDtypeStruct(x.shape, x.dtype),
        scratch_shapes=scratch_shapes,
        compiler_params=pltpu.CompilerParams(
            dimension_semantics=("arbitrary", "arbitrary"),
            vmem_limit_bytes=V7X_VMEM_LIMIT_BYTES),
        name=name,
    )(x, mod, *[arr for arr, _ in params])


def _ffn_sublayer(x, mod, norms, w_in, w_out, *, layer, which, sub, res_w, tm=1024, tf=256,
                  n_sub=4, skew=2):
    d = x.shape[2]
    d_ff = w_out.shape[2]
    body = functools.partial(_ffn_kernel, sub=sub, res_w=res_w, d_ff=d_ff, tf=tf, n_sub=n_sub,
                             skew=skew)
    params = [(norms[0], (layer * N_SUB + sub,)), (norms[1], (layer * N_SUB + sub,)),
              (w_in, (layer, which)), (w_out, (layer, which))]
    return _sublayer_call(body, "ffn_sublayer", x, mod, layer, params,
                          [pltpu.VMEM((tm, d_ff), BF16), pltpu.VMEM((tm, d), BF16)], tm)


def _split3(x):
    hi = x.astype(BF16)
    r1 = x - hi.astype(F32)
    mid = r1.astype(BF16)
    lo = (r1 - mid.astype(F32)).astype(BF16)
    return hi, mid, lo


def _rows_bcast(ref, h, rows, reps):
    pieces = [jnp.broadcast_to(ref[h, pl.ds(r, 1), :], (reps, ref.shape[2])) for r in rows]
    return pieces[0] if len(pieces) == 1 else jnp.concatenate(pieces, axis=0)


def _hg_chunk_heads(q_s, k_s, b_s, v_s, heads, r0, sts, pair_code, sel_ref):
    c = HG_CHUNK
    rows = pl.ds(r0, c)
    hs = range(len(heads))
    q = [q_s[h, rows, :] for h in heads]
    k = [k_s[h, rows, :] for h in heads]
    bc = [b_s[h, rows, :] for h in heads]
    vb = [v_s[h, rows, :] for h in heads]
    ps = [[] for _ in hs]
    for j in range(HG_BLOCK):
        src = [r0 + HG_BLOCK * n + j for n in range(c // HG_BLOCK)]
        for i in hs:
            k_j = _rows_bcast(k_s, heads[i], src, HG_BLOCK)
            b_j = _rows_bcast(b_s, heads[i], src, HG_BLOCK)
            ps[i].append((q[i] * k_j * jnp.exp2(jnp.minimum(bc[i] - b_j, 0.0))).astype(BF16))
    scores = [_dot(jnp.concatenate(ps[i], axis=1), sel_ref[...]) for i in hs]
    m = HG_BLOCK
    while m < c:
        mids = [r0 + 2 * m * g + m - 1 for g in range(c // (2 * m))]
        for i in hs:
            w = jnp.exp2(-jnp.abs(bc[i] - _rows_bcast(b_s, heads[i], mids, 2 * m)))
            qk = jnp.concatenate([(q[i] if (r // m) % 2 else k[i])[r:r + m] for r in range(0, c, m)],
                                 axis=0)
            z = (qk * w).astype(BF16)
            scores[i] = jnp.where(pair_code == m, _dot_nt(z, z), scores[i])
        m *= 2
    o = [_dot_nt((q[i] * jnp.exp2(bc[i])).astype(BF16), sts[i].astype(BF16)) for i in hs]
    st_new = []
    for i in hs:
        b_last = b_s[heads[i], pl.ds(r0 + c - 1, 1), :]
        k_dec = (k[i] * jnp.exp2(b_last - bc[i])).astype(BF16)
        st_new.append(sts[i] * jnp.exp2(b_last) + _dot_tn(vb[i], k_dec))
    for i in hs:
        a = jnp.where(pair_code == 0, 0.0, scores[i]).astype(BF16)
        o[i] = o[i] + _dot(a, vb[i])
    return o, st_new


def _hgrn_kernel(x_ref, mod_ref, gpre_ref, gpost_ref, lb_ref, onorm_ref, win_ref, wout_ref,
                 o_ref, q_s, k_s, b_s, g_s, v_s, o_s, y_s, hb_s, st_s, sel_ref,
                 *, sub, res_w, layer, lockstep, proj_heads):
    tm = x_ref.shape[1]
    n_heads, _, kd = q_s.shape
    f = n_heads * kd
    c = HG_CHUNK

    @pl.when(pl.program_id(1) == 0)
    def _():
        st_s[...] = jnp.zeros_like(st_s)

    x = x_ref[0]
    shift, scale, gate = _mod_rows(mod_ref, sub)
    hb_s[...] = _pre_norm(x, gpre_ref[...], scale, shift).astype(BF16)

    lbr = lb_ref[...]
    e = jnp.exp(lbr - jnp.max(lbr, axis=0, keepdims=True))
    lb = jnp.sum(e[:layer + 1], axis=0, keepdims=True) / jnp.sum(e, axis=0, keepdims=True)

    ri = lax.broadcasted_iota(jnp.int32, (c, c), 0)
    ci = lax.broadcasted_iota(jnp.int32, (c, c), 1)
    tri = (ri >= ci).astype(BF16)

    def project(kind, p):
        lo = kind * f + p * proj_heads * kd
        return _dot(hb_s[...], win_ref[:, lo:lo + proj_heads * kd])

    def store_heads(dst, p, val, rows=slice(None)):
        for i in range(proj_heads):
            dst[p * proj_heads + i, rows, :] = val[:, i * kd:(i + 1) * kd]

    def q_piece(p, qv):
        store_heads(q_s, p, qv * _sigmoid(qv))

    def f_piece(p, fx):
        lbp = lb[:, p * proj_heads * kd:(p + 1) * proj_heads * kd]
        en = jnp.exp(-jnp.abs(fx))
        inv = 1.0 / (1.0 + en)
        pos = fx >= 0.0
        sig_p = jnp.where(pos, inv, en * inv)
        sig_n = jnp.where(pos, en * inv, inv)
        store_heads(k_s, p, (1.0 - lbp) * sig_n)
        log2_f = jnp.log(lbp + (1.0 - lbp) * sig_p) * LOG2_E
        for j in range(tm // c):
            hi, mid, lo = _split3(log2_f[j * c:(j + 1) * c, :])
            bc = _dot(tri, hi) + _dot(tri, mid) + _dot(tri, lo)
            store_heads(b_s, p, bc, slice(j * c, (j + 1) * c))

    def v_piece(p, vv):
        store_heads(v_s, p, vv.astype(BF16))

    def g_piece(p, gv):
        store_heads(g_s, p, gv * _sigmoid(gv))

    work = [(piece, kind, p) for p in range(n_heads // proj_heads)
            for kind, piece in ((1, f_piece), (0, q_piece), (3, g_piece), (2, v_piece))]
    pending = None
    for piece, kind, p in work:
        res = project(kind, p)
        if pending is not None:
            pending()
        pending = functools.partial(piece, p, res)
    pending()

    xr = ri ^ ci
    xr = xr | (xr >> 1)
    xr = xr | (xr >> 2)
    xr = xr | (xr >> 4)
    top = xr - (xr >> 1)
    pair_code = jnp.where(ri >= ci, jnp.where(top >= HG_BLOCK, top, 1), 0)
    sel_j = lax.broadcasted_iota(jnp.int32, (HG_BLOCK * kd, c), 0) // kd
    sel_s = lax.broadcasted_iota(jnp.int32, (HG_BLOCK * kd, c), 1) & (HG_BLOCK - 1)
    sel_ref[...] = (sel_j == sel_s).astype(BF16)

    def chunk_body(j, carry):
        r0 = pl.multiple_of(j * c, c)
        for h0 in range(0, n_heads, lockstep):
            heads = list(range(h0, h0 + lockstep))
            o, st_new = _hg_chunk_heads(q_s, k_s, b_s, v_s, heads, r0, [st_s[h] for h in heads],
                                        pair_code, sel_ref)
            for i, h in enumerate(heads):
                o_s[h, pl.ds(r0, c), :] = o[i]
                st_s[h] = st_new[i]
        return carry

    lax.fori_loop(0, tm // c, chunk_body, 0)

    onorm = onorm_ref[...]
    for h in range(n_heads):
        oh = o_s[h]
        ms = jnp.mean(oh * oh, axis=-1, keepdims=True)
        y_s[:, h * kd:(h + 1) * kd] = ((oh * lax.rsqrt(ms + EPS)) * onorm * g_s[h]).astype(BF16)
    y = _dot(y_s[...], wout_ref[...])
    o_ref[0] = _post_residual(x_ref[0], y, gpost_ref[...], gate, res_w)


def _hgrn_sublayer(x, mod, norms, hg_lb, out_norm, w_in, w_out, *, layer, which, sub, res_w,
                   tm=512, lockstep=4, proj_heads=8):
    d = x.shape[2]
    n_heads = hg_lb.shape[1] // HG_KDIM
    head_buf = pltpu.VMEM((n_heads, tm, HG_KDIM), F32)
    body = functools.partial(_hgrn_kernel, sub=sub, res_w=res_w, layer=layer, lockstep=lockstep,
                             proj_heads=proj_heads)
    params = [(norms[0], (layer * N_SUB + sub,)), (norms[1], (layer * N_SUB + sub,)),
              (hg_lb, ()), (out_norm, (which,)), (w_in, (which,)), (w_out, (which,))]
    scratch = [head_buf, head_buf, head_buf, head_buf,
               pltpu.VMEM((n_heads, tm, HG_KDIM), BF16),
               head_buf,
               pltpu.VMEM((tm, d), BF16),
               pltpu.VMEM((tm, d), BF16),
               pltpu.VMEM((n_heads, HG_KDIM, HG_KDIM), F32),
               pltpu.VMEM((HG_BLOCK * HG_KDIM, HG_CHUNK), BF16)]
    return _sublayer_call(body, "hgrn_sublayer", x, mod, layer, params, scratch, tm)


GELU_C1 = 0.7978845608028654
GELU_C3 = GELU_C1 * 0.044715


def _gelu_tanh(x):
    hx = 0.5 * x
    return hx + hx * jnp.tanh(x * (GELU_C1 + GELU_C3 * (x * x)))


def _gmlp_kernel(x_ref, mod_ref, gpre_ref, gpost_ref, win_ref, bin_ref, lng_ref, lnb_ref,
                 ws_ref, bs_ref, wout_ref, o_ref, z_s, vn_s, act_s, hb_s,
                 *, sub, res_w, tn, n_sub, skew):
    tm = x_ref.shape[1]
    half = vn_s.shape[1]
    n_groups = ws_ref.shape[0]
    gdim = half // n_groups
    n_tiles = half // tn
    c = GM_CHUNK

    shift, scale, gate = _mod_rows(mod_ref, sub)
    ri = lax.broadcasted_iota(jnp.int32, (c, c), 0)
    ci = lax.broadcasted_iota(jnp.int32, (c, c), 1)
    tril = ri >= ci
    n_down = 4
    down_tn = wout_ref.shape[1] // n_down

    def row_steps(r0, nrows):
        rows = slice(r0, r0 + nrows)
        res, sums, sqs = {}, [], []

        def pre():
            hb_s[rows, :] = _pre_norm(x_ref[0, rows, :], gpre_ref[...], scale, shift).astype(BF16)

        def project(idx):
            lo = (half + idx * tn) if idx < n_tiles else (idx - n_tiles) * tn
            res[idx] = _dot(hb_s[rows, :], win_ref[:, lo:lo + tn])

        def v_act(j):
            z = _gelu_tanh(res.pop(j) + bin_ref[:, half + j * tn:half + (j + 1) * tn])
            z_s[rows, j * tn:(j + 1) * tn] = z
            sums.append(jnp.sum(z, axis=-1, keepdims=True))
            sqs.append(jnp.sum(z * z, axis=-1, keepdims=True))

        def normalize():
            mu = sum(sums) * (1.0 / half)
            var = sum(sqs) * (1.0 / half) - mu * mu
            rstd = lax.rsqrt(var + EPS)
            for j in range(n_tiles):
                sl = slice(j * tn, (j + 1) * tn)
                vn_s[rows, sl] = (((z_s[rows, sl] - mu) * rstd) * lng_ref[:, sl]
                                  + lnb_ref[:, sl]).astype(BF16)

        def u_act(j):
            u = _gelu_tanh(res.pop(n_tiles + j) + bin_ref[:, j * tn:(j + 1) * tn])
            for g in range(j * tn // gdim, (j + 1) * tn // gdim):
                wsg = jnp.where(tril, ws_ref[g], 0.0).astype(BF16)
                bsg = bs_ref[g]
                sl = slice(g * gdim, (g + 1) * gdim)
                for i in range(nrows // c):
                    rc = slice(r0 + i * c, r0 + (i + 1) * c)
                    vm = _dot(wsg, vn_s[rc, sl]) + bsg
                    ug = u[i * c:(i + 1) * c, g * gdim - j * tn:(g + 1) * gdim - j * tn]
                    act_s[rc, sl] = (ug * vm).astype(BF16)

        def step(idx):
            if idx < 2 * n_tiles:
                project(idx)
            if 1 <= idx <= n_tiles:
                v_act(idx - 1)
            if idx == n_tiles + 1:
                normalize()
            if idx > n_tiles:
                u_act(idx - n_tiles - 1)

        def down(n):
            res["y", n] = _dot(act_s[rows, :], wout_ref[:, n * down_tn:(n + 1) * down_tn])

        def post():
            y = jnp.concatenate([res.pop(("y", n)) for n in range(n_down)], axis=1)
            o_ref[0, rows, :] = _post_residual(x_ref[0, rows, :], y, gpost_ref[...], gate, res_w)

        return ([pre] + [functools.partial(step, idx) for idx in range(2 * n_tiles + 1)]
                + [functools.partial(down, n) for n in range(n_down)] + [post])

    rt = tm // n_sub
    _emit_skewed([row_steps(i * rt, rt) for i in range(n_sub)], skew)


def _gmlp_sublayer(x, mod, norms, w_in, b_in, ln_g, ln_b, w_s, b_s, w_out, *, layer, which, sub,
                   res_w, tm=512, tn=768, n_sub=2, skew=2):
    d = x.shape[2]
    half = w_out.shape[1]
    body = functools.partial(_gmlp_kernel, sub=sub, res_w=res_w, tn=tn, n_sub=n_sub, skew=skew)
    params = [(norms[0], (layer * N_SUB + sub,)), (norms[1], (layer * N_SUB + sub,)),
              (w_in, (which,)), (b_in, (which,)), (ln_g, (which,)), (ln_b, (which,)),
              (w_s, (which,)), (b_s, (which,)), (w_out, (which,))]
    scratch = [pltpu.VMEM((tm, half), F32), pltpu.VMEM((tm, half), BF16),
               pltpu.VMEM((tm, half), BF16), pltpu.VMEM((tm, d), BF16)]
    return _sublayer_call(body, "gmlp_sublayer", x, mod, layer, params, scratch, tm)


def kernel(x, c, ada_w, ada_b, norm_pre, norm_post, ffn_w_in, ffn_w_out, hg_w_in, hg_w_out,
           hg_out_norm, hg_lb, gm_w_in, gm_b_in, gm_ln_g, gm_ln_b, gm_w_s, gm_b_s, gm_w_out):
    depth, d = ada_w.shape[0], x.shape[2]
    mod = _modulation(c, ada_w, ada_b)
    norms = (norm_pre.reshape(depth * N_SUB, 1, d), norm_post.reshape(depth * N_SUB, 1, d))
    hg_out_norm = hg_out_norm[:, None, :]
    gm_b_in, gm_ln_g, gm_ln_b = gm_b_in[:, None, :], gm_ln_g[:, None, :], gm_ln_b[:, None, :]
    gm_b_s = gm_b_s[..., None]
    ffn_w_in = ffn_w_in.astype(BF16)
    ffn_w_out = ffn_w_out.astype(BF16)
    hg_w_in = hg_w_in.astype(BF16)
    hg_w_out = hg_w_out.astype(BF16)
    gm_w_in = gm_w_in.astype(BF16)
    gm_w_out = gm_w_out.astype(BF16)
    for i in range(depth):
        j = i // 2
        x = _ffn_sublayer(x, mod, norms, ffn_w_in, ffn_w_out, layer=i, which=0, sub=0, res_w=0.5)
        if i % 2 == 0:
            x = _hgrn_sublayer(x, mod, norms, hg_lb, hg_out_norm, hg_w_in, hg_w_out, layer=i,
                               which=j, sub=1, res_w=1.0)
        else:
            x = _gmlp_sublayer(x, mod, norms, gm_w_in, gm_b_in, gm_ln_g, gm_ln_b, gm_w_s, gm_b_s,
                               gm_w_out, layer=i, which=j, sub=1, res_w=1.0)
        x = _ffn_sublayer(x, mod, norms, ffn_w_in, ffn_w_out, layer=i, which=1, sub=2, res_w=0.5)
    return x
```

```python
import functools

import jax
import jax.numpy as jnp
from jax import lax
from jax.experimental import pallas as pl
from jax.experimental.pallas import tpu as pltpu

EPS = 1e-6
LOG2_E = 1.4426950408889634
N_SUB = 3
HG_KDIM = 128
HG_CHUNK = 128
HG_BLOCK = 8
GM_CHUNK = 128
GM_GROUPS = 8

V7X_VMEM_LIMIT_BYTES = 56 * 1024 * 1024

F32 = jnp.float32
BF16 = jnp.bfloat16


def _dot(a, b):
    return jnp.dot(a, b, preferred_element_type=F32)


def _dot_nt(a, b):
    return lax.dot_general(a, b, (((1,), (1,)), ((), ())), preferred_element_type=F32)


def _dot_tn(a, b):
    return lax.dot_general(a, b, (((0,), (0,)), ((), ())), preferred_element_type=F32)


def _sigmoid(x):
    return 1.0 / (1.0 + jnp.exp(-x))


def _pre_norm(x, g, scale, shift):
    ms = jnp.mean(x * x, axis=-1, keepdims=True)
    return (x * lax.rsqrt(ms + EPS)) * g * (1.0 + scale) + shift


def _post_residual(x, y, g, gate, res_w):
    ms = jnp.mean(y * y, axis=-1, keepdims=True)
    return x + (res_w * gate) * ((y * lax.rsqrt(ms + EPS)) * g)


def _mod_rows(mod_ref, sub):
    shift = mod_ref[0, 3 * sub:3 * sub + 1, :]
    scale = mod_ref[0, 3 * sub + 1:3 * sub + 2, :]
    gate = mod_ref[0, 3 * sub + 2:3 * sub + 3, :]
    return shift, scale, gate


def _mod_kernel(ct_ref, w_ref, b_ref, o_ref, *, n_batch):
    ct = ct_ref[...]
    cond = ct * _sigmoid(ct)
    w = w_ref[0]
    rows = [jnp.sum(w * cond[:, b:b + 1], axis=0, keepdims=True) + b_ref[0]
            for b in range(n_batch)]
    pad = jnp.zeros((o_ref.shape[1] - n_batch, w.shape[1]), F32)
    o_ref[0] = jnp.concatenate(rows + [pad], axis=0)


def _modulation(c, ada_w, ada_b):
    depth, d, n = ada_w.shape
    b = c.shape[0]
    rows = 8
    tn = 1024
    out = pl.pallas_call(
        functools.partial(_mod_kernel, n_batch=b),
        grid=(depth, n // tn),
        in_specs=[
            pl.BlockSpec((d, b), lambda i, j: (0, 0)),
            pl.BlockSpec((1, d, tn), lambda i, j: (i, 0, j)),
            pl.BlockSpec((1, 1, tn), lambda i, j: (i, 0, j)),
        ],
        out_specs=pl.BlockSpec((1, rows, tn), lambda i, j: (i, 0, j)),
        out_shape=jax.ShapeDtypeStruct((depth, rows, n), F32),
        compiler_params=pltpu.CompilerParams(
            dimension_semantics=("arbitrary", "arbitrary")),
        name="adaln_mod",
    )(c.T, ada_w, ada_b.reshape(depth, 1, n))
    return out.reshape(depth, rows, 3 * N_SUB, d)


def _emit_skewed(streams, skew):
    n = max(len(s) + i * skew for i, s in enumerate(streams))
    for k in range(n):
        for i, s in enumerate(streams):
            if 0 <= k - i * skew < len(s):
                s[k - i * skew]()


def _ffn_kernel(x_ref, mod_ref, gpre_ref, gpost_ref, win_ref, wout_ref, o_ref, act_s, hb_s,
                *, sub, res_w, d_ff, tf, n_sub, skew):
    tm = x_ref.shape[1]
    shift, scale, gate = _mod_rows(mod_ref, sub)

    def row_steps(rows):
        res = {}

        def pre():
            hb_s[rows, :] = _pre_norm(x_ref[0, rows, :], gpre_ref[...], scale, shift).astype(BF16)

        def up(j):
            if j < d_ff // tf:
                hb = hb_s[rows, :]
                res[j] = (_dot(hb, win_ref[:, j * tf:(j + 1) * tf]),
                          _dot(hb, win_ref[:, d_ff + j * tf:d_ff + (j + 1) * tf]))
            if j > 0:
                a, b = res.pop(j - 1)
                act_s[rows, (j - 1) * tf:j * tf] = (a * _sigmoid(a) * b).astype(BF16)

        def down():
            res["y"] = _dot(act_s[rows, :], wout_ref[...])

        def post():
            o_ref[0, rows, :] = _post_residual(x_ref[0, rows, :], res.pop("y"), gpost_ref[...],
                                               gate, res_w)

        return [pre] + [functools.partial(up, j) for j in range(d_ff // tf + 1)] + [down, post]

    rt = tm // n_sub
    _emit_skewed([row_steps(slice(i * rt, (i + 1) * rt)) for i in range(n_sub)], skew)


def _sublayer_call(body, name, x, mod, layer, params, scratch_shapes, tm, lag=0):
    bsz, s, d = x.shape
    n_tiles = s // tm

    def picked(arr, idx):
        rest = (0,) * (arr.ndim - len(idx))
        return pl.BlockSpec((None,) * len(idx) + arr.shape[len(idx):],
                            lambda b, i: idx + rest, pipeline_mode=pl.Buffered(1))

    return pl.pallas_call(
        body,
        grid=(bsz, n_tiles + lag),
        in_specs=[pl.BlockSpec((1, tm, d), lambda b, i: (b, jnp.minimum(i, n_tiles - 1), 0)),
                  pl.BlockSpec((None, 1, 3 * N_SUB, d), lambda b, i: (layer, b, 0, 0))]
        + [picked(arr, idx) for arr, idx in params],
        out_specs=pl.BlockSpec((1, tm, d), lambda b, i: (b, jnp.maximum(i - lag, 0), 0)),
        out_shape=jax.ShapeDtypeStruct(x.shape, x.dtype),
        scratch_shapes=scratch_shapes,
        compiler_params=pltpu.CompilerParams(
            dimension_semantics=("arbitrary", "arbitrary"),
            vmem_limit_bytes=V7X_VMEM_LIMIT_BYTES),
        name=name,
    )(x, mod, *[arr for arr, _ in params])


def _ffn_sublayer(x, mod, norms, w_in, w_out, *, layer, which, sub, res_w, tm=1024, tf=256,
                  n_sub=4, skew=2):
    d = x.shape[2]
    d_ff = w_out.shape[2]
    body = functools.partial(_ffn_kernel, sub=sub, res_w=res_w, d_ff=d_ff, tf=tf, n_sub=n_sub,
                             skew=skew)
    params = [(norms[0], (layer * N_SUB + sub,)), (norms[1], (layer * N_SUB + sub,)),
              (w_in, (layer, which)), (w_out, (layer, which))]
    return _sublayer_call(body, "ffn_sublayer", x, mod, layer, params,
                          [pltpu.VMEM((tm, d_ff), BF16), pltpu.VMEM((tm, d), BF16)], tm)


def _split3(x):
    hi = x.astype(BF16)
    r1 = x - hi.astype(F32)
    mid = r1.astype(BF16)
    lo = (r1 - mid.astype(F32)).astype(BF16)
    return hi, mid, lo


def _rows_bcast(ref, h, rows, reps):
    pieces = [jnp.broadcast_to(ref[h, pl.ds(r, 1), :], (reps, ref.shape[2])) for r in rows]
    return pieces[0] if len(pieces) == 1 else jnp.concatenate(pieces, axis=0)


def _hg_chunk_heads(q_s, k_s, b_s, v_s, heads, r0, sts, pair_code, sel_ref):
    c = HG_CHUNK
    rows = pl.ds(r0, c)
    hs = range(len(heads))
    q = [q_s[h, rows, :] for h in heads]
    k = [k_s[h, rows, :] for h in heads]
    bc = [b_s[h, rows, :] for h in heads]
    vb = [v_s[h, rows, :] for h in heads]
    ps = [[] for _ in hs]
    for j in range(HG_BLOCK):
        src = [r0 + HG_BLOCK * n + j for n in range(c // HG_BLOCK)]
        for i in hs:
            if j == HG_BLOCK - 1:
                ps[i].append((q[i] * k[i]).astype(BF16))
                continue
            k_j = _rows_bcast(k_s, heads[i], src, HG_BLOCK)
            b_j = _rows_bcast(b_s, heads[i], src, HG_BLOCK)
            ps[i].append((q[i] * k_j * jnp.exp2(jnp.minimum(bc[i] - b_j, 0.0))).astype(BF16))
    scores = [_dot(jnp.concatenate(ps[i], axis=1), sel_ref[...]) for i in hs]
    m = HG_BLOCK
    while m < c:
        mids = [r0 + 2 * m * g + m - 1 for g in range(c // (2 * m))]
        for i in hs:
            w = jnp.exp2(-jnp.abs(bc[i] - _rows_bcast(b_s, heads[i], mids, 2 * m)))
            qk = jnp.concatenate([(q[i] if (r // m) % 2 else k[i])[r:r + m] for r in range(0, c, m)],
                                 axis=0)
            z = (qk * w).astype(BF16)
            scores[i] = jnp.where(pair_code == m, _dot_nt(z, z), scores[i])
        m *= 2
    o = [_dot_nt((q[i] * jnp.exp2(bc[i])).astype(BF16), sts[i].astype(BF16)) for i in hs]
    st_new = []
    for i in hs:
        b_last = b_s[heads[i], pl.ds(r0 + c - 1, 1), :]
        k_dec = (k[i] * jnp.exp2(b_last - bc[i])).astype(BF16)
        st_new.append(sts[i] * jnp.exp2(b_last) + _dot_tn(vb[i], k_dec))
    for i in hs:
        a = jnp.where(pair_code == 0, 0.0, scores[i]).astype(BF16)
        o[i] = o[i] + _dot(a, vb[i])
    return o, st_new


def _hgrn_kernel(x_ref, mod_ref, gpre_ref, gpost_ref, lb_ref, onorm_ref, win_ref, wout_ref,
                 o_ref, q_s, k_s, b_s, g_s, v_s, o_s, y_s, hb_s, st_s, sel_ref,
                 *, sub, res_w, layer, lockstep, proj_heads, n_sub, skew):
    tm = x_ref.shape[1]
    n_heads, _, kd = q_s.shape
    f = n_heads * kd
    c = HG_CHUNK

    @pl.when(pl.program_id(1) == 0)
    def _():
        st_s[...] = jnp.zeros_like(st_s)

    shift, scale, gate = _mod_rows(mod_ref, sub)

    lbr = lb_ref[...]
    e = jnp.exp(lbr - jnp.max(lbr, axis=0, keepdims=True))
    lb = jnp.sum(e[:layer + 1], axis=0, keepdims=True) / jnp.sum(e, axis=0, keepdims=True)

    ri = lax.broadcasted_iota(jnp.int32, (c, c), 0)
    ci = lax.broadcasted_iota(jnp.int32, (c, c), 1)
    tri = (ri >= ci).astype(BF16)

    def projection_steps(r0, nrows):
        rows = slice(r0, r0 + nrows)

        def pre():
            hb_s[rows, :] = _pre_norm(x_ref[0, rows, :], gpre_ref[...], scale, shift).astype(BF16)

        def store_heads(dst, p, val, rr=rows):
            for i in range(proj_heads):
                dst[p * proj_heads + i, rr, :] = val[:, i * kd:(i + 1) * kd]

        def q_piece(p, qv):
            store_heads(q_s, p, qv * _sigmoid(qv))

        def f_piece(p, fx):
            lbp = lb[:, p * proj_heads * kd:(p + 1) * proj_heads * kd]
            en = jnp.exp(-jnp.abs(fx))
            inv = 1.0 / (1.0 + en)
            pos = fx >= 0.0
            sig_p = jnp.where(pos, inv, en * inv)
            sig_n = jnp.where(pos, en * inv, inv)
            store_heads(k_s, p, (1.0 - lbp) * sig_n)
            log2_f = jnp.log(lbp + (1.0 - lbp) * sig_p) * LOG2_E
            for j in range(nrows // c):
                hi, mid, lo = _split3(log2_f[j * c:(j + 1) * c, :])
                bc = _dot(tri, hi) + _dot(tri, mid) + _dot(tri, lo)
                store_heads(b_s, p, bc, slice(r0 + j * c, r0 + (j + 1) * c))

        def v_piece(p, vv):
            store_heads(v_s, p, vv.astype(BF16))

        def g_piece(p, gv):
            store_heads(g_s, p, gv * _sigmoid(gv))

        work = [(piece, kind, p) for p in range(n_heads // proj_heads)
                for kind, piece in ((1, f_piece), (0, q_piece), (3, g_piece), (2, v_piece))]
        res = {}

        def step(n):
            if n < len(work):
                _, kind, p = work[n]
                lo = kind * f + p * proj_heads * kd
                res[n] = _dot(hb_s[rows, :], win_ref[:, lo:lo + proj_heads * kd])
            if n > 0:
                piece, _, p = work[n - 1]
                piece(p, res.pop(n - 1))

        return [pre] + [functools.partial(step, n) for n in range(len(work) + 1)]

    rt = tm // n_sub
    _emit_skewed([projection_steps(i * rt, rt) for i in range(n_sub)], skew)

    xr = ri ^ ci
    xr = xr | (xr >> 1)
    xr = xr | (xr >> 2)
    xr = xr | (xr >> 4)
    top = xr - (xr >> 1)
    pair_code = jnp.where(ri >= ci, jnp.where(top >= HG_BLOCK, top, 1), 0)
    sel_j = lax.broadcasted_iota(jnp.int32, (HG_BLOCK * kd, c), 0) // kd
    sel_s = lax.broadcasted_iota(jnp.int32, (HG_BLOCK * kd, c), 1) & (HG_BLOCK - 1)
    sel_ref[...] = (sel_j == sel_s).astype(BF16)

    def chunk_body(j, carry):
        r0 = pl.multiple_of(j * c, c)
        for h0 in range(0, n_heads, lockstep):
            heads = list(range(h0, h0 + lockstep))
            o, st_new = _hg_chunk_heads(q_s, k_s, b_s, v_s, heads, r0, [st_s[h] for h in heads],
                                        pair_code, sel_ref)
            for i, h in enumerate(heads):
                o_s[h, pl.ds(r0, c), :] = o[i]
                st_s[h] = st_new[i]
        return carry

    lax.fori_loop(0, tm // c, chunk_body, 0)

    def output_steps(r0, nrows):
        rows = slice(r0, r0 + nrows)
        res = {}

        def head_norm():
            onorm = onorm_ref[...]
            for h in range(n_heads):
                oh = o_s[h, rows, :]
                ms = jnp.mean(oh * oh, axis=-1, keepdims=True)
                y_s[rows, h * kd:(h + 1) * kd] = ((oh * lax.rsqrt(ms + EPS)) * onorm
                                                  * g_s[h, rows, :]).astype(BF16)

        def down():
            res["y"] = _dot(y_s[rows, :], wout_ref[...])

        def post():
            o_ref[0, rows, :] = _post_residual(x_ref[0, rows, :], res.pop("y"), gpost_ref[...],
                                               gate, res_w)

        return [head_norm, down, post]

    _emit_skewed([output_steps(i * rt, rt) for i in range(n_sub)], 1)


def _hgrn_sublayer(x, mod, norms, hg_lb, out_norm, w_in, w_out, *, layer, which, sub, res_w,
                   tm=512, lockstep=4, proj_heads=4, n_sub=2, skew=2):
    d = x.shape[2]
    n_heads = hg_lb.shape[1] // HG_KDIM
    head_buf = pltpu.VMEM((n_heads, tm, HG_KDIM), F32)
    body = functools.partial(_hgrn_kernel, sub=sub, res_w=res_w, layer=layer, lockstep=lockstep,
                             proj_heads=proj_heads, n_sub=n_sub, skew=skew)
    params = [(norms[0], (layer * N_SUB + sub,)), (norms[1], (layer * N_SUB + sub,)),
              (hg_lb, ()), (out_norm, (which,)), (w_in, (which,)), (w_out, (which,))]
    scratch = [head_buf, head_buf, head_buf, head_buf,
               pltpu.VMEM((n_heads, tm, HG_KDIM), BF16),
               head_buf,
               pltpu.VMEM((tm, d), BF16),
               pltpu.VMEM((tm, d), BF16),
               pltpu.VMEM((n_heads, HG_KDIM, HG_KDIM), F32),
               pltpu.VMEM((HG_BLOCK * HG_KDIM, HG_CHUNK), BF16)]
    return _sublayer_call(body, "hgrn_sublayer", x, mod, layer, params, scratch, tm)


GELU_C1 = 0.7978845608028654
GELU_C3 = GELU_C1 * 0.044715


def _gelu_tanh(x):
    hx = 0.5 * x
    return hx + hx * jnp.tanh(x * (GELU_C1 + GELU_C3 * (x * x)))


def _gmlp_kernel(x_ref, mod_ref, gpre_ref, gpost_ref, win_ref, bin_ref, lng_ref, lnb_ref,
                 ws_ref, bs_ref, wout_ref, o_ref, z_s, vn_s, act_s, hb_s,
                 *, sub, res_w, tn, n_sub, skew):
    tm = x_ref.shape[1]
    half = vn_s.shape[1]
    n_groups = ws_ref.shape[0]
    gdim = half // n_groups
    n_tiles = half // tn
    c = GM_CHUNK

    shift, scale, gate = _mod_rows(mod_ref, sub)
    ri = lax.broadcasted_iota(jnp.int32, (c, c), 0)
    ci = lax.broadcasted_iota(jnp.int32, (c, c), 1)
    tril = ri >= ci
    n_down = 4
    down_tn = wout_ref.shape[1] // n_down

    def row_steps(r0, nrows):
        rows = slice(r0, r0 + nrows)
        res, sums, sqs = {}, [], []

        def pre():
            hb_s[rows, :] = _pre_norm(x_ref[0, rows, :], gpre_ref[...], scale, shift).astype(BF16)

        def project(idx):
            lo = (half + idx * tn) if idx < n_tiles else (idx - n_tiles) * tn
            res[idx] = _dot(hb_s[rows, :], win_ref[:, lo:lo + tn])

        def v_act(j):
            z = _gelu_tanh(res.pop(j) + bin_ref[:, half + j * tn:half + (j + 1) * tn])
            z_s[rows, j * tn:(j + 1) * tn] = z
            sums.append(jnp.sum(z, axis=-1, keepdims=True))
            sqs.append(jnp.sum(z * z, axis=-1, keepdims=True))

        def normalize():
            mu = sum(sums) * (1.0 / half)
            var = sum(sqs) * (1.0 / half) - mu * mu
            rstd = lax.rsqrt(var + EPS)
            for j in range(n_tiles):
                sl = slice(j * tn, (j + 1) * tn)
                vn_s[rows, sl] = (((z_s[rows, sl] - mu) * rstd) * lng_ref[:, sl]
                                  + lnb_ref[:, sl]).astype(BF16)

        def u_act(j):
            u = _gelu_tanh(res.pop(n_tiles + j) + bin_ref[:, j * tn:(j + 1) * tn])
            for g in range(j * tn // gdim, ((j + 1) * tn - 1) // gdim + 1):
                lo, hi = max(g * gdim, j * tn), min((g + 1) * gdim, (j + 1) * tn)
                wsg = jnp.where(tril, ws_ref[g], 0.0).astype(BF16)
                bsg = bs_ref[g]
                for i in range(nrows // c):
                    rc = slice(r0 + i * c, r0 + (i + 1) * c)
                    vm = _dot(wsg, vn_s[rc, lo:hi]) + bsg
                    ug = u[i * c:(i + 1) * c, lo - j * tn:hi - j * tn]
                    act_s[rc, lo:hi] = (ug * vm).astype(BF16)

        def step(idx):
            if idx < 2 * n_tiles:
                project(idx)
            if 1 <= idx <= n_tiles:
                v_act(idx - 1)
            if idx == n_tiles + 1:
                normalize()
            if idx > n_tiles:
                u_act(idx - n_tiles - 1)

        def down(n):
            res["y", n] = _dot(act_s[rows, :], wout_ref[:, n * down_tn:(n + 1) * down_tn])

        def post():
            y = jnp.concatenate([res.pop(("y", n)) for n in range(n_down)], axis=1)
            o_ref[0, rows, :] = _post_residual(x_ref[0, rows, :], y, gpost_ref[...], gate, res_w)

        return ([pre] + [functools.partial(step, idx) for idx in range(2 * n_tiles + 1)]
                + [functools.partial(down, n) for n in range(n_down)] + [post])

    rt = tm // n_sub
    _emit_skewed([row_steps(i * rt, rt) for i in range(n_sub)], skew)


def _gmlp_sublayer(x, mod, norms, w_in, b_in, ln_g, ln_b, w_s, b_s, w_out, *, layer, which, sub,
                   res_w, tm=512, tn=256, n_sub=2, skew=2):
    d = x.shape[2]
    half = w_out.shape[1]
    body = functools.partial(_gmlp_kernel, sub=sub, res_w=res_w, tn=tn, n_sub=n_sub, skew=skew)
    params = [(norms[0], (layer * N_SUB + sub,)), (norms[1], (layer * N_SUB + sub,)),
              (w_in, (which,)), (b_in, (which,)), (ln_g, (which,)), (ln_b, (which,)),
              (w_s, (which,)), (b_s, (which,)), (w_out, (which,))]
    scratch = [pltpu.VMEM((tm, half), F32), pltpu.VMEM((tm, half), BF16),
               pltpu.VMEM((tm, half), BF16), pltpu.VMEM((tm, d), BF16)]
    return _sublayer_call(body, "gmlp_sublayer", x, mod, layer, params, scratch, tm)


def kernel(x, c, ada_w, ada_b, norm_pre, norm_post, ffn_w_in, ffn_w_out, hg_w_in, hg_w_out,
           hg_out_norm, hg_lb, gm_w_in, gm_b_in, gm_ln_g, gm_ln_b, gm_w_s, gm_b_s, gm_w_out):
    depth, d = ada_w.shape[0], x.shape[2]
    mod = _modulation(c, ada_w, ada_b)
    norms = (norm_pre.reshape(depth * N_SUB, 1, d), norm_post.reshape(depth * N_SUB, 1, d))
    hg_out_norm = hg_out_norm[:, None, :]
    gm_b_in, gm_ln_g, gm_ln_b = gm_b_in[:, None, :], gm_ln_g[:, None, :], gm_ln_b[:, None, :]
    gm_b_s = gm_b_s[..., None]
    ffn_w_in = ffn_w_in.astype(BF16)
    ffn_w_out = ffn_w_out.astype(BF16)
    hg_w_in = hg_w_in.astype(BF16)
    hg_w_out = hg_w_out.astype(BF16)
    gm_w_in = gm_w_in.astype(BF16)
    gm_w_out = gm_w_out.astype(BF16)
    for i in range(depth):
        j = i // 2
        x = _ffn_sublayer(x, mod, norms, ffn_w_in, ffn_w_out, layer=i, which=0, sub=0, res_w=0.5)
        if i % 2 == 0:
            x = _hgrn_sublayer(x, mod, norms, hg_lb, hg_out_norm, hg_w_in, hg_w_out, layer=i,
                               which=j, sub=1, res_w=1.0)
        else:
            x = _gmlp_sublayer(x, mod, norms, gm_w_in, gm_b_in, gm_ln_g, gm_ln_b, gm_w_s, gm_b_s,
                               gm_w_out, layer=i, which=j, sub=1, res_w=1.0)
        x = _ffn_sublayer(x, mod, norms, ffn_w_in, ffn_w_out, layer=i, which=1, sub=2, res_w=0.5)
    return x
```

```python
import functools

import jax
import jax.numpy as jnp
from jax import lax
from jax.experimental import pallas as pl
from jax.experimental.pallas import tpu as pltpu

EPS = 1e-6
LOG2_E = 1.4426950408889634
N_SUB = 3
HG_KDIM = 128
HG_CHUNK = 128
HG_BLOCK = 8
GM_CHUNK = 128
GM_GROUPS = 8

V7X_VMEM_LIMIT_BYTES = 56 * 1024 * 1024

F32 = jnp.float32
BF16 = jnp.bfloat16


def _dot(a, b):
    return jnp.dot(a, b, preferred_element_type=F32)


def _dot_nt(a, b):
    return lax.dot_general(a, b, (((1,), (1,)), ((), ())), preferred_element_type=F32)


def _dot_tn(a, b):
    return lax.dot_general(a, b, (((0,), (0,)), ((), ())), preferred_element_type=F32)


def _sigmoid(x):
    return 1.0 / (1.0 + jnp.exp(-x))


def _pre_norm(x, g, scale, shift):
    ms = jnp.mean(x * x, axis=-1, keepdims=True)
    return (x * lax.rsqrt(ms + EPS)) * g * (1.0 + scale) + shift


def _post_residual(x, y, g, gate, res_w):
    ms = jnp.mean(y * y, axis=-1, keepdims=True)
    return x + (res_w * gate) * ((y * lax.rsqrt(ms + EPS)) * g)


def _mod_rows(mod_ref, sub):
    shift = mod_ref[0, 3 * sub:3 * sub + 1, :]
    scale = mod_ref[0, 3 * sub + 1:3 * sub + 2, :]
    gate = mod_ref[0, 3 * sub + 2:3 * sub + 3, :]
    return shift, scale, gate


def _mod_kernel(ct_ref, w_ref, b_ref, o_ref, *, n_batch):
    ct = ct_ref[...]
    cond = ct * _sigmoid(ct)
    w = w_ref[0]
    rows = [jnp.sum(w * cond[:, b:b + 1], axis=0, keepdims=True) + b_ref[0]
            for b in range(n_batch)]
    pad = jnp.zeros((o_ref.shape[1] - n_batch, w.shape[1]), F32)
    o_ref[0] = jnp.concatenate(rows + [pad], axis=0)


def _modulation(c, ada_w, ada_b):
    depth, d, n = ada_w.shape
    b = c.shape[0]
    rows = 8
    tn = 1024
    out = pl.pallas_call(
        functools.partial(_mod_kernel, n_batch=b),
        grid=(depth, n // tn),
        in_specs=[
            pl.BlockSpec((d, b), lambda i, j: (0, 0)),
            pl.BlockSpec((1, d, tn), lambda i, j: (i, 0, j)),
            pl.BlockSpec((1, 1, tn), lambda i, j: (i, 0, j)),
        ],
        out_specs=pl.BlockSpec((1, rows, tn), lambda i, j: (i, 0, j)),
        out_shape=jax.ShapeDtypeStruct((depth, rows, n), F32),
        compiler_params=pltpu.CompilerParams(
            dimension_semantics=("arbitrary", "arbitrary")),
        name="adaln_mod",
    )(c.T, ada_w, ada_b.reshape(depth, 1, n))
    return out.reshape(depth, rows, 3 * N_SUB, d)


def _emit_skewed(streams, skew):
    n = max(len(s) + i * skew for i, s in enumerate(streams))
    for k in range(n):
        for i, s in enumerate(streams):
            if 0 <= k - i * skew < len(s):
                s[k - i * skew]()


def _ffn_kernel(x_ref, mod_ref, gpre_ref, gpost_ref, win_ref, wout_ref, o_ref, act_s, hb_s,
                *, sub, res_w, d_ff, tf, n_sub, skew):
    tm = x_ref.shape[1]
    shift, scale, gate = _mod_rows(mod_ref, sub)

    def row_steps(rows):
        res = {}

        def pre():
            hb_s[rows, :] = _pre_norm(x_ref[0, rows, :], gpre_ref[...], scale, shift).astype(BF16)

        def up(j):
            if j < d_ff // tf:
                hb = hb_s[rows, :]
                res[j] = (_dot(hb, win_ref[:, j * tf:(j + 1) * tf]),
                          _dot(hb, win_ref[:, d_ff + j * tf:d_ff + (j + 1) * tf]))
            if j > 0:
                a, b = res.pop(j - 1)
                act_s[rows, (j - 1) * tf:j * tf] = (a * _sigmoid(a) * b).astype(BF16)

        def down():
            res["y"] = _dot(act_s[rows, :], wout_ref[...])

        def post():
            o_ref[0, rows, :] = _post_residual(x_ref[0, rows, :], res.pop("y"), gpost_ref[...],
                                               gate, res_w)

        return [pre] + [functools.partial(up, j) for j in range(d_ff // tf + 1)] + [down, post]

    rt = tm // n_sub
    _emit_skewed([row_steps(slice(i * rt, (i + 1) * rt)) for i in range(n_sub)], skew)


def _sublayer_call(body, name, x, mod, layer, params, scratch_shapes, tm, lag=0):
    bsz, s, d = x.shape
    n_tiles = s // tm

    def picked(arr, idx):
        rest = (0,) * (arr.ndim - len(idx))
        return pl.BlockSpec((None,) * len(idx) + arr.shape[len(idx):],
                            lambda b, i: idx + rest, pipeline_mode=pl.Buffered(1))

    return pl.pallas_call(
        body,
        grid=(bsz, n_tiles + lag),
        in_specs=[pl.BlockSpec((1, tm, d), lambda b, i: (b, jnp.minimum(i, n_tiles - 1), 0)),
                  pl.BlockSpec((None, 1, 3 * N_SUB, d), lambda b, i: (layer, b, 0, 0))]
        + [picked(arr, idx) for arr, idx in params],
        out_specs=pl.BlockSpec((1, tm, d), lambda b, i: (b, jnp.maximum(i - lag, 0), 0)),
        out_shape=jax.ShapeDtypeStruct(x.shape, x.dtype),
        scratch_shapes=scratch_shapes,
        compiler_params=pltpu.CompilerParams(
            dimension_semantics=("arbitrary", "arbitrary"),
            vmem_limit_bytes=V7X_VMEM_LIMIT_BYTES),
        name=name,
    )(x, mod, *[arr for arr, _ in params])


def _ffn_sublayer(x, mod, norms, w_in, w_out, *, layer, which, sub, res_w, tm=1024, tf=256,
                  n_sub=4, skew=2):
    d = x.shape[2]
    d_ff = w_out.shape[2]
    body = functools.partial(_ffn_kernel, sub=sub, res_w=res_w, d_ff=d_ff, tf=tf, n_sub=n_sub,
                             skew=skew)
    params = [(norms[0], (layer * N_SUB + sub,)), (norms[1], (layer * N_SUB + sub,)),
              (w_in, (layer, which)), (w_out, (layer, which))]
    return _sublayer_call(body, "ffn_sublayer", x, mod, layer, params,
                          [pltpu.VMEM((tm, d_ff), BF16), pltpu.VMEM((tm, d), BF16)], tm)


def _split3(x):
    hi = x.astype(BF16)
    r1 = x - hi.astype(F32)
    mid = r1.astype(BF16)
    lo = (r1 - mid.astype(F32)).astype(BF16)
    return hi, mid, lo


def _rows_bcast(ref, h, rows, reps):
    pieces = [jnp.broadcast_to(ref[h, pl.ds(r, 1), :], (reps, ref.shape[2])) for r in rows]
    return pieces[0] if len(pieces) == 1 else jnp.concatenate(pieces, axis=0)


def _hg_chunk_heads(q_s, k_s, b_s, v_s, heads, r0, sts, pair_code, sel_ref):
    c = HG_CHUNK
    rows = pl.ds(r0, c)
    hs = range(len(heads))
    q = [q_s[h, rows, :] for h in heads]
    k = [k_s[h, rows, :] for h in heads]
    bc = [b_s[h, rows, :] for h in heads]
    vb = [v_s[h, rows, :] for h in heads]
    ps = [[] for _ in hs]
    for j in range(HG_BLOCK):
        src = [r0 + HG_BLOCK * n + j for n in range(c // HG_BLOCK)]
        for i in hs:
            if j == HG_BLOCK - 1:
                ps[i].append((q[i] * k[i]).astype(BF16))
                continue
            k_j = _rows_bcast(k_s, heads[i], src, HG_BLOCK)
            b_j = _rows_bcast(b_s, heads[i], src, HG_BLOCK)
            ps[i].append((q[i] * k_j * jnp.exp2(jnp.minimum(bc[i] - b_j, 0.0))).astype(BF16))
    stacked = _dot(jnp.concatenate([jnp.concatenate(ps[i], axis=1) for i in hs], axis=0),
                   sel_ref[...])
    scores = [stacked[i * c:(i + 1) * c] for i in hs]
    m = HG_BLOCK
    while m < c:
        mids = [r0 + 2 * m * g + m - 1 for g in range(c // (2 * m))]
        for i in hs:
            b_mid = _rows_bcast(b_s, heads[i], mids, 2 * m)
            dec = jnp.concatenate([(bc[i][r:r + m] - b_mid[r:r + m]) if (r // m) % 2
                                   else (b_mid[r:r + m] - bc[i][r:r + m])
                                   for r in range(0, c, m)], axis=0)
            qk = jnp.concatenate([(q[i] if (r // m) % 2 else k[i])[r:r + m] for r in range(0, c, m)],
                                 axis=0)
            z = (qk * jnp.exp2(dec)).astype(BF16)
            scores[i] = jnp.where(pair_code == m, _dot_nt(z, z), scores[i])
        m *= 2
    o = [_dot_nt((q[i] * jnp.exp2(bc[i])).astype(BF16), sts[i].astype(BF16)) for i in hs]
    st_new = []
    for i in hs:
        b_last = b_s[heads[i], pl.ds(r0 + c - 1, 1), :]
        k_dec = (k[i] * jnp.exp2(b_last - bc[i])).astype(BF16)
        st_new.append(sts[i] * jnp.exp2(b_last) + _dot_tn(vb[i], k_dec))
    for i in hs:
        a = jnp.where(pair_code == 0, 0.0, scores[i]).astype(BF16)
        o[i] = o[i] + _dot(a, vb[i])
    return o, st_new


def _hgrn_kernel(x_ref, mod_ref, gpre_ref, gpost_ref, lb_ref, onorm_ref, win_ref, wout_ref,
                 o_ref, q_s, k_s, b_s, g_s, v_s, o_s, y_s, hb_s, st_s, sel_ref,
                 *, sub, res_w, layer, lockstep, proj_heads, n_sub, skew):
    tm = x_ref.shape[1]
    n_heads, _, kd = q_s.shape
    f = n_heads * kd
    c = HG_CHUNK

    @pl.when(pl.program_id(1) == 0)
    def _():
        st_s[...] = jnp.zeros_like(st_s)

    shift, scale, gate = _mod_rows(mod_ref, sub)

    lbr = lb_ref[...]
    e = jnp.exp(lbr - jnp.max(lbr, axis=0, keepdims=True))
    lb = jnp.sum(e[:layer + 1], axis=0, keepdims=True) / jnp.sum(e, axis=0, keepdims=True)

    ri = lax.broadcasted_iota(jnp.int32, (c, c), 0)
    ci = lax.broadcasted_iota(jnp.int32, (c, c), 1)
    tri = (ri >= ci).astype(BF16)

    def projection_steps(r0, nrows):
        rows = slice(r0, r0 + nrows)

        def pre():
            hb_s[rows, :] = _pre_norm(x_ref[0, rows, :], gpre_ref[...], scale, shift).astype(BF16)

        def store_heads(dst, p, val, rr=rows):
            for i in range(proj_heads):
                dst[p * proj_heads + i, rr, :] = val[:, i * kd:(i + 1) * kd]

        def q_piece(p, qv):
            store_heads(q_s, p, qv * _sigmoid(qv))

        def f_piece(p, fx):
            lbp = lb[:, p * proj_heads * kd:(p + 1) * proj_heads * kd]
            en = jnp.exp(-jnp.abs(fx))
            inv = 1.0 / (1.0 + en)
            pos = fx >= 0.0
            sig_p = jnp.where(pos, inv, en * inv)
            sig_n = jnp.where(pos, en * inv, inv)
            store_heads(k_s, p, (1.0 - lbp) * sig_n)
            log2_f = jnp.log(lbp + (1.0 - lbp) * sig_p) * LOG2_E
            for j in range(nrows // c):
                hi, mid, lo = _split3(log2_f[j * c:(j + 1) * c, :])
                bc = _dot(tri, hi) + _dot(tri, mid) + _dot(tri, lo)
                store_heads(b_s, p, bc, slice(r0 + j * c, r0 + (j + 1) * c))

        def v_piece(p, vv):
            store_heads(v_s, p, vv.astype(BF16))

        def g_piece(p, gv):
            store_heads(g_s, p, gv * _sigmoid(gv))

        work = [(piece, kind, p) for p in range(n_heads // proj_heads)
                for kind, piece in ((1, f_piece), (0, q_piece), (3, g_piece), (2, v_piece))]
        res = {}

        def step(n):
            if n < len(work):
                _, kind, p = work[n]
                lo = kind * f + p * proj_heads * kd
                res[n] = _dot(hb_s[rows, :], win_ref[:, lo:lo + proj_heads * kd])
            if n > 0:
                piece, _, p = work[n - 1]
                piece(p, res.pop(n - 1))

        return [pre] + [functools.partial(step, n) for n in range(len(work) + 1)]

    rt = tm // n_sub
    _emit_skewed([projection_steps(i * rt, rt) for i in range(n_sub)], skew)

    xr = ri ^ ci
    xr = xr | (xr >> 1)
    xr = xr | (xr >> 2)
    xr = xr | (xr >> 4)
    top = xr - (xr >> 1)
    pair_code = jnp.where(ri >= ci, jnp.where(top >= HG_BLOCK, top, 1), 0)
    sel_j = lax.broadcasted_iota(jnp.int32, (HG_BLOCK * kd, c), 0) // kd
    sel_s = lax.broadcasted_iota(jnp.int32, (HG_BLOCK * kd, c), 1) & (HG_BLOCK - 1)
    sel_ref[...] = (sel_j == sel_s).astype(BF16)

    def chunk_body(j, carry):
        r0 = pl.multiple_of(j * c, c)
        for h0 in range(0, n_heads, lockstep):
            heads = list(range(h0, h0 + lockstep))
            o, st_new = _hg_chunk_heads(q_s, k_s, b_s, v_s, heads, r0, [st_s[h] for h in heads],
                                        pair_code, sel_ref)
            for i, h in enumerate(heads):
                o_s[h, pl.ds(r0, c), :] = o[i]
                st_s[h] = st_new[i]
        return carry

    lax.fori_loop(0, tm // c, chunk_body, 0)

    def output_steps(r0, nrows):
        rows = slice(r0, r0 + nrows)
        res = {}

        def head_norm():
            onorm = onorm_ref[...]
            for h in range(n_heads):
                oh = o_s[h, rows, :]
                ms = jnp.mean(oh * oh, axis=-1, keepdims=True)
                y_s[rows, h * kd:(h + 1) * kd] = ((oh * lax.rsqrt(ms + EPS)) * onorm
                                                  * g_s[h, rows, :]).astype(BF16)

        def down():
            res["y"] = _dot(y_s[rows, :], wout_ref[...])

        def post():
            o_ref[0, rows, :] = _post_residual(x_ref[0, rows, :], res.pop("y"), gpost_ref[...],
                                               gate, res_w)

        return [head_norm, down, post]

    _emit_skewed([output_steps(i * rt, rt) for i in range(n_sub)], 1)


def _hgrn_sublayer(x, mod, norms, hg_lb, out_norm, w_in, w_out, *, layer, which, sub, res_w,
                   tm=512, lockstep=8, proj_heads=8, n_sub=1, skew=2):
    d = x.shape[2]
    n_heads = hg_lb.shape[1] // HG_KDIM
    head_buf = pltpu.VMEM((n_heads, tm, HG_KDIM), F32)
    body = functools.partial(_hgrn_kernel, sub=sub, res_w=res_w, layer=layer, lockstep=lockstep,
                             proj_heads=proj_heads, n_sub=n_sub, skew=skew)
    params = [(norms[0], (layer * N_SUB + sub,)), (norms[1], (layer * N_SUB + sub,)),
              (hg_lb, ()), (out_norm, (which,)), (w_in, (which,)), (w_out, (which,))]
    scratch = [head_buf, head_buf, head_buf, head_buf,
               pltpu.VMEM((n_heads, tm, HG_KDIM), BF16),
               head_buf,
               pltpu.VMEM((tm, d), BF16),
               pltpu.VMEM((tm, d), BF16),
               pltpu.VMEM((n_heads, HG_KDIM, HG_KDIM), F32),
               pltpu.VMEM((HG_BLOCK * HG_KDIM, HG_CHUNK), BF16)]
    return _sublayer_call(body, "hgrn_sublayer", x, mod, layer, params, scratch, tm)


GELU_C1 = 0.7978845608028654
GELU_C3 = GELU_C1 * 0.044715


def _gelu_tanh(x):
    hx = 0.5 * x
    return hx + hx * jnp.tanh(x * (GELU_C1 + GELU_C3 * (x * x)))


def _gmlp_kernel(x_ref, mod_ref, gpre_ref, gpost_ref, win_ref, bin_ref, lng_ref, lnb_ref,
                 ws_ref, bs_ref, wout_ref, o_ref, z_s, vn_s, act_s, hb_s,
                 *, sub, res_w, tn, n_sub, skew):
    tm = x_ref.shape[1]
    half = vn_s.shape[1]
    n_groups = ws_ref.shape[0]
    gdim = half // n_groups
    n_tiles = half // tn
    c = GM_CHUNK

    shift, scale, gate = _mod_rows(mod_ref, sub)
    ri = lax.broadcasted_iota(jnp.int32, (c, c), 0)
    ci = lax.broadcasted_iota(jnp.int32, (c, c), 1)
    tril = ri >= ci
    n_down = 4
    down_tn = wout_ref.shape[1] // n_down

    def row_steps(r0, nrows):
        rows = slice(r0, r0 + nrows)
        res, sums, sqs = {}, [], []

        def pre():
            hb_s[rows, :] = _pre_norm(x_ref[0, rows, :], gpre_ref[...], scale, shift).astype(BF16)

        def project(idx):
            lo = (half + idx * tn) if idx < n_tiles else (idx - n_tiles) * tn
            res[idx] = _dot(hb_s[rows, :], win_ref[:, lo:lo + tn])

        def v_act(j):
            z = _gelu_tanh(res.pop(j) + bin_ref[:, half + j * tn:half + (j + 1) * tn])
            z_s[rows, j * tn:(j + 1) * tn] = z
            sums.append(jnp.sum(z, axis=-1, keepdims=True))
            sqs.append(jnp.sum(z * z, axis=-1, keepdims=True))

        def normalize():
            mu = sum(sums) * (1.0 / half)
            var = sum(sqs) * (1.0 / half) - mu * mu
            rstd = lax.rsqrt(var + EPS)
            for j in range(n_tiles):
                sl = slice(j * tn, (j + 1) * tn)
                vn_s[rows, sl] = (((z_s[rows, sl] - mu) * rstd) * lng_ref[:, sl]
                                  + lnb_ref[:, sl]).astype(BF16)

        def u_act(j):
            u = _gelu_tanh(res.pop(n_tiles + j) + bin_ref[:, j * tn:(j + 1) * tn])
            for g in range(j * tn // gdim, ((j + 1) * tn - 1) // gdim + 1):
                lo, hi = max(g * gdim, j * tn), min((g + 1) * gdim, (j + 1) * tn)
                wsg = jnp.where(tril, ws_ref[g], 0.0).astype(BF16)
                bsg = bs_ref[g]
                for i in range(nrows // c):
                    rc = slice(r0 + i * c, r0 + (i + 1) * c)
                    vm = _dot(wsg, vn_s[rc, lo:hi]) + bsg
                    ug = u[i * c:(i + 1) * c, lo - j * tn:hi - j * tn]
                    act_s[rc, lo:hi] = (ug * vm).astype(BF16)

        def step(idx):
            if idx < 2 * n_tiles:
                project(idx)
            if 1 <= idx <= n_tiles:
                v_act(idx - 1)
            if idx == n_tiles + 1:
                normalize()
            if idx > n_tiles:
                u_act(idx - n_tiles - 1)

        def down(n):
            res["y", n] = _dot(act_s[rows, :], wout_ref[:, n * down_tn:(n + 1) * down_tn])

        def post():
            y = jnp.concatenate([res.pop(("y", n)) for n in range(n_down)], axis=1)
            o_ref[0, rows, :] = _post_residual(x_ref[0, rows, :], y, gpost_ref[...], gate, res_w)

        return ([pre] + [functools.partial(step, idx) for idx in range(2 * n_tiles + 1)]
                + [functools.partial(down, n) for n in range(n_down)] + [post])

    rt = tm // n_sub
    _emit_skewed([row_steps(i * rt, rt) for i in range(n_sub)], skew)


def _gmlp_sublayer(x, mod, norms, w_in, b_in, ln_g, ln_b, w_s, b_s, w_out, *, layer, which, sub,
                   res_w, tm=512, tn=256, n_sub=1, skew=2):
    d = x.shape[2]
    half = w_out.shape[1]
    body = functools.partial(_gmlp_kernel, sub=sub, res_w=res_w, tn=tn, n_sub=n_sub, skew=skew)
    params = [(norms[0], (layer * N_SUB + sub,)), (norms[1], (layer * N_SUB + sub,)),
              (w_in, (which,)), (b_in, (which,)), (ln_g, (which,)), (ln_b, (which,)),
              (w_s, (which,)), (b_s, (which,)), (w_out, (which,))]
    scratch = [pltpu.VMEM((tm, half), F32), pltpu.VMEM((tm, half), BF16),
               pltpu.VMEM((tm, half), BF16), pltpu.VMEM((tm, d), BF16)]
    return _sublayer_call(body, "gmlp_sublayer", x, mod, layer, params, scratch, tm)


def kernel(x, c, ada_w, ada_b, norm_pre, norm_post, ffn_w_in, ffn_w_out, hg_w_in, hg_w_out,
           hg_out_norm, hg_lb, gm_w_in, gm_b_in, gm_ln_g, gm_ln_b, gm_w_s, gm_b_s, gm_w_out):
    depth, d = ada_w.shape[0], x.shape[2]
    mod = _modulation(c, ada_w, ada_b)
    norms = (norm_pre.reshape(depth * N_SUB, 1, d), norm_post.reshape(depth * N_SUB, 1, d))
    hg_out_norm = hg_out_norm[:, None, :]
    gm_b_in, gm_ln_g, gm_ln_b = gm_b_in[:, None, :], gm_ln_g[:, None, :], gm_ln_b[:, None, :]
    gm_b_s = gm_b_s[..., None]
    ffn_w_in = ffn_w_in.astype(BF16)
    ffn_w_out = ffn_w_out.astype(BF16)
    hg_w_in = hg_w_in.astype(BF16)
    hg_w_out = hg_w_out.astype(BF16)
    gm_w_in = gm_w_in.astype(BF16)
    gm_w_out = gm_w_out.astype(BF16)
    for i in range(depth):
        j = i // 2
        x = _ffn_sublayer(x, mod, norms, ffn_w_in, ffn_w_out, layer=i, which=0, sub=0, res_w=0.5)
        if i % 2 == 0:
            x = _hgrn_sublayer(x, mod, norms, hg_lb, hg_out_norm, hg_w_in, hg_w_out, layer=i,
                               which=j, sub=1, res_w=1.0)
        else:
            x = _gmlp_sublayer(x, mod, norms, gm_w_in, gm_b_in, gm_ln_g, gm_ln_b, gm_w_s, gm_b_s,
                               gm_w_out, layer=i, which=j, sub=1, res_w=1.0)
        x = _ffn_sublayer(x, mod, norms, ffn_w_in, ffn_w_out, layer=i, which=1, sub=2, res_w=0.5)
    return x
```

```python
import functools

import jax
import jax.numpy as jnp
from jax import lax
from jax.experimental import pallas as pl
from jax.experimental.pallas import tpu as pltpu

EPS = 1e-6
LOG2_E = 1.4426950408889634
N_SUB = 3
HG_KDIM = 128
HG_CHUNK = 128
HG_BLOCK = 8
GM_CHUNK = 128
GM_GROUPS = 8

V7X_VMEM_LIMIT_BYTES = 56 * 1024 * 1024

F32 = jnp.float32
BF16 = jnp.bfloat16


def _dot(a, b):
    return jnp.dot(a, b, preferred_element_type=F32)


def _dot_nt(a, b):
    return lax.dot_general(a, b, (((1,), (1,)), ((), ())), preferred_element_type=F32)


def _dot_tn(a, b):
    return lax.dot_general(a, b, (((0,), (0,)), ((), ())), preferred_element_type=F32)


def _sigmoid(x):
    return 1.0 / (1.0 + jnp.exp(-x))


def _pre_norm(x, g, scale, shift):
    ms = jnp.mean(x * x, axis=-1, keepdims=True)
    return (x * lax.rsqrt(ms + EPS)) * g * (1.0 + scale) + shift


def _post_residual(x, y, g, gate, res_w):
    ms = jnp.mean(y * y, axis=-1, keepdims=True)
    return x + (res_w * gate) * ((y * lax.rsqrt(ms + EPS)) * g)


def _mod_rows(mod_ref, sub):
    shift = mod_ref[0, 3 * sub:3 * sub + 1, :]
    scale = mod_ref[0, 3 * sub + 1:3 * sub + 2, :]
    gate = mod_ref[0, 3 * sub + 2:3 * sub + 3, :]
    return shift, scale, gate


def _mod_kernel(ct_ref, w_ref, b_ref, o_ref, *, n_batch):
    ct = ct_ref[...]
    cond = ct * _sigmoid(ct)
    w = w_ref[0]
    rows = [jnp.sum(w * cond[:, b:b + 1], axis=0, keepdims=True) + b_ref[0]
            for b in range(n_batch)]
    pad = jnp.zeros((o_ref.shape[1] - n_batch, w.shape[1]), F32)
    o_ref[0] = jnp.concatenate(rows + [pad], axis=0)


def _modulation(c, ada_w, ada_b):
    depth, d, n = ada_w.shape
    b = c.shape[0]
    rows = 8
    tn = 1024
    out = pl.pallas_call(
        functools.partial(_mod_kernel, n_batch=b),
        grid=(depth, n // tn),
        in_specs=[
            pl.BlockSpec((d, b), lambda i, j: (0, 0)),
            pl.BlockSpec((1, d, tn), lambda i, j: (i, 0, j)),
            pl.BlockSpec((1, 1, tn), lambda i, j: (i, 0, j)),
        ],
        out_specs=pl.BlockSpec((1, rows, tn), lambda i, j: (i, 0, j)),
        out_shape=jax.ShapeDtypeStruct((depth, rows, n), F32),
        compiler_params=pltpu.CompilerParams(
            dimension_semantics=("arbitrary", "arbitrary")),
        name="adaln_mod",
    )(c.T, ada_w, ada_b.reshape(depth, 1, n))
    return out.reshape(depth, rows, 3 * N_SUB, d)


def _emit_skewed(streams, skew):
    n = max(len(s) + i * skew for i, s in enumerate(streams))
    for k in range(n):
        for i, s in enumerate(streams):
            if 0 <= k - i * skew < len(s):
                s[k - i * skew]()


def _ffn_kernel(x_ref, mod_ref, gpre_ref, gpost_ref, win_ref, wout_ref, o_ref, act_s, hb_s,
                *, sub, res_w, d_ff, tf, n_sub, skew):
    tm = x_ref.shape[1]
    shift, scale, gate = _mod_rows(mod_ref, sub)

    def row_steps(rows):
        res = {}

        def pre():
            hb_s[rows, :] = _pre_norm(x_ref[0, rows, :], gpre_ref[...], scale, shift).astype(BF16)

        def up(j):
            if j < d_ff // tf:
                hb = hb_s[rows, :]
                res[j] = (_dot(hb, win_ref[:, j * tf:(j + 1) * tf]),
                          _dot(hb, win_ref[:, d_ff + j * tf:d_ff + (j + 1) * tf]))
            if j > 0:
                a, b = res.pop(j - 1)
                act_s[rows, (j - 1) * tf:j * tf] = (a * _sigmoid(a) * b).astype(BF16)

        def down():
            res["y"] = _dot(act_s[rows, :], wout_ref[...])

        def post():
            o_ref[0, rows, :] = _post_residual(x_ref[0, rows, :], res.pop("y"), gpost_ref[...],
                                               gate, res_w)

        return [pre] + [functools.partial(up, j) for j in range(d_ff // tf + 1)] + [down, post]

    rt = tm // n_sub
    _emit_skewed([row_steps(slice(i * rt, (i + 1) * rt)) for i in range(n_sub)], skew)


def _sublayer_call(body, name, x, mod, layer, params, scratch_shapes, tm, lag=0):
    bsz, s, d = x.shape
    n_tiles = s // tm

    def picked(arr, idx):
        rest = (0,) * (arr.ndim - len(idx))
        return pl.BlockSpec((None,) * len(idx) + arr.shape[len(idx):],
                            lambda b, i: idx + rest, pipeline_mode=pl.Buffered(1))

    return pl.pallas_call(
        body,
        grid=(bsz, n_tiles + lag),
        in_specs=[pl.BlockSpec((1, tm, d), lambda b, i: (b, jnp.minimum(i, n_tiles - 1), 0)),
                  pl.BlockSpec((None, 1, 3 * N_SUB, d), lambda b, i: (layer, b, 0, 0))]
        + [picked(arr, idx) for arr, idx in params],
        out_specs=pl.BlockSpec((1, tm, d), lambda b, i: (b, jnp.maximum(i - lag, 0), 0)),
        out_shape=jax.ShapeDtypeStruct(x.shape, x.dtype),
        scratch_shapes=scratch_shapes,
        compiler_params=pltpu.CompilerParams(
            dimension_semantics=("arbitrary", "arbitrary"),
            vmem_limit_bytes=V7X_VMEM_LIMIT_BYTES),
        name=name,
    )(x, mod, *[arr for arr, _ in params])


def _ffn_sublayer(x, mod, norms, w_in, w_out, *, layer, which, sub, res_w, tm=1024, tf=256,
                  n_sub=4, skew=2):
    d = x.shape[2]
    d_ff = w_out.shape[2]
    body = functools.partial(_ffn_kernel, sub=sub, res_w=res_w, d_ff=d_ff, tf=tf, n_sub=n_sub,
                             skew=skew)
    params = [(norms[0], (layer * N_SUB + sub,)), (norms[1], (layer * N_SUB + sub,)),
              (w_in, (layer, which)), (w_out, (layer, which))]
    return _sublayer_call(body, "ffn_sublayer", x, mod, layer, params,
                          [pltpu.VMEM((tm, d_ff), BF16), pltpu.VMEM((tm, d), BF16)], tm)


def _split3(x):
    hi = x.astype(BF16)
    r1 = x - hi.astype(F32)
    mid = r1.astype(BF16)
    lo = (r1 - mid.astype(F32)).astype(BF16)
    return hi, mid, lo


def _rows_bcast(ref, h, rows, reps):
    pieces = [jnp.broadcast_to(ref[h, pl.ds(r, 1), :], (reps, ref.shape[2])) for r in rows]
    return pieces[0] if len(pieces) == 1 else jnp.concatenate(pieces, axis=0)


def _hg_chunk_heads(q_s, k_s, b_s, v_s, heads, r0, sts, pair_code, sel_ref):
    c = HG_CHUNK
    rows = pl.ds(r0, c)
    hs = range(len(heads))
    q = [q_s[h, rows, :] for h in heads]
    k = [k_s[h, rows, :] for h in heads]
    bc = [b_s[h, rows, :] for h in heads]
    vb = [v_s[h, rows, :] for h in heads]
    ps = [[] for _ in hs]
    for j in range(HG_BLOCK):
        src = [r0 + HG_BLOCK * n + j for n in range(c // HG_BLOCK)]
        for i in hs:
            if j == HG_BLOCK - 1:
                ps[i].append((q[i] * k[i]).astype(BF16))
                continue
            k_j = _rows_bcast(k_s, heads[i], src, HG_BLOCK)
            b_j = _rows_bcast(b_s, heads[i], src, HG_BLOCK)
            ps[i].append((q[i] * k_j * jnp.exp2(jnp.minimum(bc[i] - b_j, 0.0))).astype(BF16))
    stacked = _dot(jnp.concatenate([jnp.concatenate(ps[i], axis=1) for i in hs], axis=0),
                   sel_ref[...])
    scores = [stacked[i * c:(i + 1) * c] for i in hs]
    m = HG_BLOCK
    while m < c:
        mids = [r0 + 2 * m * g + m - 1 for g in range(c // (2 * m))]
        for i in hs:
            b_mid = _rows_bcast(b_s, heads[i], mids, 2 * m)
            dec = jnp.concatenate([(bc[i][r:r + m] - b_mid[r:r + m]) if (r // m) % 2
                                   else (b_mid[r:r + m] - bc[i][r:r + m])
                                   for r in range(0, c, m)], axis=0)
            qk = jnp.concatenate([(q[i] if (r // m) % 2 else k[i])[r:r + m] for r in range(0, c, m)],
                                 axis=0)
            z = (qk * jnp.exp2(dec)).astype(BF16)
            scores[i] = jnp.where(pair_code == m, _dot_nt(z, z), scores[i])
        m *= 2
    st_new = []
    for i in hs:
        b_last = b_s[heads[i], pl.ds(r0 + c - 1, 1), :]
        k_dec = (k[i] * jnp.exp2(b_last - bc[i])).astype(BF16)
        st_new.append(sts[i] * jnp.exp2(b_last) + _dot_tn(vb[i], k_dec))
    o = []
    for i in hs:
        a = jnp.where(pair_code == 0, 0.0, scores[i]).astype(BF16)
        qe = (q[i] * jnp.exp2(bc[i])).astype(BF16)
        o.append(_dot(jnp.concatenate([a, qe], axis=1),
                      jnp.concatenate([vb[i], sts[i].astype(BF16).T], axis=0)))
    return o, st_new


def _hgrn_kernel(x_ref, mod_ref, gpre_ref, gpost_ref, lb_ref, onorm_ref, win_ref, wout_ref,
                 o_ref, q_s, k_s, b_s, g_s, v_s, o_s, y_s, hb_s, st_s, sel_ref,
                 *, sub, res_w, layer, lockstep, proj_heads, n_sub, skew):
    tm = x_ref.shape[1]
    n_heads, _, kd = q_s.shape
    f = n_heads * kd
    c = HG_CHUNK

    @pl.when(pl.program_id(1) == 0)
    def _():
        st_s[...] = jnp.zeros_like(st_s)

    shift, scale, gate = _mod_rows(mod_ref, sub)

    lbr = lb_ref[...]
    e = jnp.exp(lbr - jnp.max(lbr, axis=0, keepdims=True))
    lb = jnp.sum(e[:layer + 1], axis=0, keepdims=True) / jnp.sum(e, axis=0, keepdims=True)

    ri = lax.broadcasted_iota(jnp.int32, (c, c), 0)
    ci = lax.broadcasted_iota(jnp.int32, (c, c), 1)
    tri = (ri >= ci).astype(BF16)

    def projection_steps(r0, nrows):
        rows = slice(r0, r0 + nrows)

        def pre():
            hb_s[rows, :] = _pre_norm(x_ref[0, rows, :], gpre_ref[...], scale, shift).astype(BF16)

        def store_heads(dst, p, val, rr=rows):
            for i in range(proj_heads):
                dst[p * proj_heads + i, rr, :] = val[:, i * kd:(i + 1) * kd]

        def q_piece(p, qv):
            store_heads(q_s, p, qv * _sigmoid(qv))

        def f_piece(p, fx):
            lbp = lb[:, p * proj_heads * kd:(p + 1) * proj_heads * kd]
            en = jnp.exp(-jnp.abs(fx))
            inv = 1.0 / (1.0 + en)
            pos = fx >= 0.0
            sig_p = jnp.where(pos, inv, en * inv)
            sig_n = jnp.where(pos, en * inv, inv)
            store_heads(k_s, p, (1.0 - lbp) * sig_n)
            log2_f = jnp.log(lbp + (1.0 - lbp) * sig_p) * LOG2_E
            for j in range(nrows // c):
                hi, mid, lo = _split3(log2_f[j * c:(j + 1) * c, :])
                bc = _dot(tri, hi) + _dot(tri, mid) + _dot(tri, lo)
                store_heads(b_s, p, bc, slice(r0 + j * c, r0 + (j + 1) * c))

        def v_piece(p, vv):
            store_heads(v_s, p, vv.astype(BF16))

        def g_piece(p, gv):
            store_heads(g_s, p, gv * _sigmoid(gv))

        work = [(piece, kind, p) for p in range(n_heads // proj_heads)
                for kind, piece in ((1, f_piece), (0, q_piece), (3, g_piece), (2, v_piece))]
        res = {}

        def step(n):
            if n < len(work):
                _, kind, p = work[n]
                lo = kind * f + p * proj_heads * kd
                res[n] = _dot(hb_s[rows, :], win_ref[:, lo:lo + proj_heads * kd])
            if n > 0:
                piece, _, p = work[n - 1]
                piece(p, res.pop(n - 1))

        return [pre] + [functools.partial(step, n) for n in range(len(work) + 1)]

    rt = tm // n_sub
    _emit_skewed([projection_steps(i * rt, rt) for i in range(n_sub)], skew)

    xr = ri ^ ci
    xr = xr | (xr >> 1)
    xr = xr | (xr >> 2)
    xr = xr | (xr >> 4)
    top = xr - (xr >> 1)
    pair_code = jnp.where(ri >= ci, jnp.where(top >= HG_BLOCK, top, 1), 0)
    sel_j = lax.broadcasted_iota(jnp.int32, (HG_BLOCK * kd, c), 0) // kd
    sel_s = lax.broadcasted_iota(jnp.int32, (HG_BLOCK * kd, c), 1) & (HG_BLOCK - 1)
    sel_ref[...] = (sel_j == sel_s).astype(BF16)

    def chunk_body(j, carry):
        r0 = pl.multiple_of(j * c, c)
        for h0 in range(0, n_heads, lockstep):
            heads = list(range(h0, h0 + lockstep))
            o, st_new = _hg_chunk_heads(q_s, k_s, b_s, v_s, heads, r0, [st_s[h] for h in heads],
                                        pair_code, sel_ref)
            for i, h in enumerate(heads):
                o_s[h, pl.ds(r0, c), :] = o[i]
                st_s[h] = st_new[i]
        return carry

    lax.fori_loop(0, tm // c, chunk_body, 0)

    def output_steps(r0, nrows):
        rows = slice(r0, r0 + nrows)
        res = {}

        def head_norm():
            onorm = onorm_ref[...]
            for h in range(n_heads):
                oh = o_s[h, rows, :]
                ms = jnp.mean(oh * oh, axis=-1, keepdims=True)
                y_s[rows, h * kd:(h + 1) * kd] = ((oh * lax.rsqrt(ms + EPS)) * onorm
                                                  * g_s[h, rows, :]).astype(BF16)

        def down():
            res["y"] = _dot(y_s[rows, :], wout_ref[...])

        def post():
            o_ref[0, rows, :] = _post_residual(x_ref[0, rows, :], res.pop("y"), gpost_ref[...],
                                               gate, res_w)

        return [head_norm, down, post]

    _emit_skewed([output_steps(i * rt, rt) for i in range(n_sub)], 1)


def _hgrn_sublayer(x, mod, norms, hg_lb, out_norm, w_in, w_out, *, layer, which, sub, res_w,
                   tm=512, lockstep=8, proj_heads=8, n_sub=1, skew=2):
    d = x.shape[2]
    n_heads = hg_lb.shape[1] // HG_KDIM
    head_buf = pltpu.VMEM((n_heads, tm, HG_KDIM), F32)
    body = functools.partial(_hgrn_kernel, sub=sub, res_w=res_w, layer=layer, lockstep=lockstep,
                             proj_heads=proj_heads, n_sub=n_sub, skew=skew)
    params = [(norms[0], (layer * N_SUB + sub,)), (norms[1], (layer * N_SUB + sub,)),
              (hg_lb, ()), (out_norm, (which,)), (w_in, (which,)), (w_out, (which,))]
    scratch = [head_buf, head_buf, head_buf, head_buf,
               pltpu.VMEM((n_heads, tm, HG_KDIM), BF16),
               head_buf,
               pltpu.VMEM((tm, d), BF16),
               pltpu.VMEM((tm, d), BF16),
               pltpu.VMEM((n_heads, HG_KDIM, HG_KDIM), F32),
               pltpu.VMEM((HG_BLOCK * HG_KDIM, HG_CHUNK), BF16)]
    return _sublayer_call(body, "hgrn_sublayer", x, mod, layer, params, scratch, tm)


GELU_C1 = 0.7978845608028654
GELU_C3 = GELU_C1 * 0.044715


def _gelu_tanh(x):
    hx = 0.5 * x
    return hx + hx * jnp.tanh(x * (GELU_C1 + GELU_C3 * (x * x)))


def _gmlp_kernel(x_ref, mod_ref, gpre_ref, gpost_ref, win_ref, bin_ref, lng_ref, lnb_ref,
                 ws_ref, bs_ref, wout_ref, o_ref, z_s, vn_s, act_s, hb_s,
                 *, sub, res_w, tn, n_sub, skew):
    tm = x_ref.shape[1]
    half = vn_s.shape[1]
    n_groups = ws_ref.shape[0]
    gdim = half // n_groups
    n_tiles = half // tn
    c = GM_CHUNK

    shift, scale, gate = _mod_rows(mod_ref, sub)
    ri = lax.broadcasted_iota(jnp.int32, (c, c), 0)
    ci = lax.broadcasted_iota(jnp.int32, (c, c), 1)
    tril = ri >= ci
    n_down = 4
    down_tn = wout_ref.shape[1] // n_down

    def row_steps(r0, nrows):
        rows = slice(r0, r0 + nrows)
        res, sums, sqs = {}, [], []

        def pre():
            hb_s[rows, :] = _pre_norm(x_ref[0, rows, :], gpre_ref[...], scale, shift).astype(BF16)

        def project(idx):
            lo = (half + idx * tn) if idx < n_tiles else (idx - n_tiles) * tn
            res[idx] = _dot(hb_s[rows, :], win_ref[:, lo:lo + tn])

        def v_act(j):
            z = _gelu_tanh(res.pop(j) + bin_ref[:, half + j * tn:half + (j + 1) * tn])
            z_s[rows, j * tn:(j + 1) * tn] = z
            sums.append(jnp.sum(z, axis=-1, keepdims=True))
            sqs.append(jnp.sum(z * z, axis=-1, keepdims=True))

        def normalize():
            mu = sum(sums) * (1.0 / half)
            var = sum(sqs) * (1.0 / half) - mu * mu
            rstd = lax.rsqrt(var + EPS)
            for j in range(n_tiles):
                sl = slice(j * tn, (j + 1) * tn)
                vn_s[rows, sl] = (((z_s[rows, sl] - mu) * rstd) * lng_ref[:, sl]
                                  + lnb_ref[:, sl]).astype(BF16)

        def u_act(j):
            u = _gelu_tanh(res.pop(n_tiles + j) + bin_ref[:, j * tn:(j + 1) * tn])
            for g in range(j * tn // gdim, ((j + 1) * tn - 1) // gdim + 1):
                lo, hi = max(g * gdim, j * tn), min((g + 1) * gdim, (j + 1) * tn)
                wsg = jnp.where(tril, ws_ref[g], 0.0).astype(BF16)
                bsg = bs_ref[g]
                for i in range(nrows // c):
                    rc = slice(r0 + i * c, r0 + (i + 1) * c)
                    vm = _dot(wsg, vn_s[rc, lo:hi]) + bsg
                    ug = u[i * c:(i + 1) * c, lo - j * tn:hi - j * tn]
                    act_s[rc, lo:hi] = (ug * vm).astype(BF16)

        def step(idx):
            if idx < 2 * n_tiles:
                project(idx)
            if 1 <= idx <= n_tiles:
                v_act(idx - 1)
            if idx == n_tiles + 1:
                normalize()
            if idx > n_tiles:
                u_act(idx - n_tiles - 1)

        def down(n):
            res["y", n] = _dot(act_s[rows, :], wout_ref[:, n * down_tn:(n + 1) * down_tn])

        def post():
            y = jnp.concatenate([res.pop(("y", n)) for n in range(n_down)], axis=1)
            o_ref[0, rows, :] = _post_residual(x_ref[0, rows, :], y, gpost_ref[...], gate, res_w)

        return ([pre] + [functools.partial(step, idx) for idx in range(2 * n_tiles + 1)]
                + [functools.partial(down, n) for n in range(n_down)] + [post])

    rt = tm // n_sub
    _emit_skewed([row_steps(i * rt, rt) for i in range(n_sub)], skew)


def _gmlp_sublayer(x, mod, norms, w_in, b_in, ln_g, ln_b, w_s, b_s, w_out, *, layer, which, sub,
                   res_w, tm=512, tn=256, n_sub=1, skew=2):
    d = x.shape[2]
    half = w_out.shape[1]
    body = functools.partial(_gmlp_kernel, sub=sub, res_w=res_w, tn=tn, n_sub=n_sub, skew=skew)
    params = [(norms[0], (layer * N_SUB + sub,)), (norms[1], (layer * N_SUB + sub,)),
              (w_in, (which,)), (b_in, (which,)), (ln_g, (which,)), (ln_b, (which,)),
              (w_s, (which,)), (b_s, (which,)), (w_out, (which,))]
    scratch = [pltpu.VMEM((tm, half), F32), pltpu.VMEM((tm, half), BF16),
               pltpu.VMEM((tm, half), BF16), pltpu.VMEM((tm, d), BF16)]
    return _sublayer_call(body, "gmlp_sublayer", x, mod, layer, params, scratch, tm)


def kernel(x, c, ada_w, ada_b, norm_pre, norm_post, ffn_w_in, ffn_w_out, hg_w_in, hg_w_out,
           hg_out_norm, hg_lb, gm_w_in, gm_b_in, gm_ln_g, gm_ln_b, gm_w_s, gm_b_s, gm_w_out):
    depth, d = ada_w.shape[0], x.shape[2]
    mod = _modulation(c, ada_w, ada_b)
    norms = (norm_pre.reshape(depth * N_SUB, 1, d), norm_post.reshape(depth * N_SUB, 1, d))
    hg_out_norm = hg_out_norm[:, None, :]
    gm_b_in, gm_ln_g, gm_ln_b = gm_b_in[:, None, :], gm_ln_g[:, None, :], gm_ln_b[:, None, :]
    gm_b_s = gm_b_s[..., None]
    ffn_w_in = ffn_w_in.astype(BF16)
    ffn_w_out = ffn_w_out.astype(BF16)
    hg_w_in = hg_w_in.astype(BF16)
    hg_w_out = hg_w_out.astype(BF16)
    gm_w_in = gm_w_in.astype(BF16)
    gm_w_out = gm_w_out.astype(BF16)
    for i in range(depth):
        j = i // 2
        x = _ffn_sublayer(x, mod, norms, ffn_w_in, ffn_w_out, layer=i, which=0, sub=0, res_w=0.5)
        if i % 2 == 0:
            x = _hgrn_sublayer(x, mod, norms, hg_lb, hg_out_norm, hg_w_in, hg_w_out, layer=i,
                               which=j, sub=1, res_w=1.0)
        else:
            x = _gmlp_sublayer(x, mod, norms, gm_w_in, gm_b_in, gm_ln_g, gm_ln_b, gm_w_s, gm_b_s,
                               gm_w_out, layer=i, which=j, sub=1, res_w=1.0)
        x = _ffn_sublayer(x, mod, norms, ffn_w_in, ffn_w_out, layer=i, which=1, sub=2, res_w=0.5)
    return x
```

```python
import functools

import jax
import jax.numpy as jnp
from jax import lax
from jax.experimental import pallas as pl
from jax.experimental.pallas import tpu as pltpu

EPS = 1e-6
LOG2_E = 1.4426950408889634
N_SUB = 3
HG_KDIM = 128
HG_CHUNK = 128
HG_BLOCK = 8
GM_CHUNK = 128
GM_GROUPS = 8

V7X_VMEM_LIMIT_BYTES = 56 * 1024 * 1024

F32 = jnp.float32
BF16 = jnp.bfloat16


def _dot(a, b):
    return jnp.dot(a, b, preferred_element_type=F32)


def _dot_nt(a, b):
    return lax.dot_general(a, b, (((1,), (1,)), ((), ())), preferred_element_type=F32)


def _dot_tn(a, b):
    return lax.dot_general(a, b, (((0,), (0,)), ((), ())), preferred_element_type=F32)


def _silu(x):
    hx = 0.5 * x
    return hx + hx * jnp.tanh(hx)


def _pre_norm(x, g, scale, shift):
    ms = jnp.mean(x * x, axis=-1, keepdims=True)
    return (x * lax.rsqrt(ms + EPS)) * (g * (1.0 + scale)) + shift


def _post_residual(x, y, g, gate, res_w):
    ms = jnp.mean(y * y, axis=-1, keepdims=True)
    return x + (y * lax.rsqrt(ms + EPS)) * ((res_w * gate) * g)


def _mod_rows(mod_ref, sub):
    shift = mod_ref[0, 3 * sub:3 * sub + 1, :]
    scale = mod_ref[0, 3 * sub + 1:3 * sub + 2, :]
    gate = mod_ref[0, 3 * sub + 2:3 * sub + 3, :]
    return shift, scale, gate


def _mod_kernel(ct_ref, w_ref, b_ref, o_ref, *, n_batch):
    ct = ct_ref[...]
    cond = _silu(ct)
    w = w_ref[0]
    rows = [jnp.sum(w * cond[:, b:b + 1], axis=0, keepdims=True) + b_ref[0]
            for b in range(n_batch)]
    pad = jnp.zeros((o_ref.shape[1] - n_batch, w.shape[1]), F32)
    o_ref[0] = jnp.concatenate(rows + [pad], axis=0)


def _modulation(c, ada_w, ada_b):
    depth, d, n = ada_w.shape
    b = c.shape[0]
    rows = 8
    tn = 1024
    out = pl.pallas_call(
        functools.partial(_mod_kernel, n_batch=b),
        grid=(depth, n // tn),
        in_specs=[
            pl.BlockSpec((d, b), lambda i, j: (0, 0)),
            pl.BlockSpec((1, d, tn), lambda i, j: (i, 0, j)),
            pl.BlockSpec((1, 1, tn), lambda i, j: (i, 0, j)),
        ],
        out_specs=pl.BlockSpec((1, rows, tn), lambda i, j: (i, 0, j)),
        out_shape=jax.ShapeDtypeStruct((depth, rows, n), F32),
        compiler_params=pltpu.CompilerParams(
            dimension_semantics=("arbitrary", "arbitrary")),
        name="adaln_mod",
    )(c.T, ada_w, ada_b.reshape(depth, 1, n))
    return out.reshape(depth, rows, 3 * N_SUB, d)


def _emit_skewed(streams, skew):
    n = max(len(s) + i * skew for i, s in enumerate(streams))
    for k in range(n):
        for i, s in enumerate(streams):
            if 0 <= k - i * skew < len(s):
                s[k - i * skew]()


def _ffn_kernel(x_ref, mod_ref, gpre_ref, gpost_ref, win_ref, wout_ref, o_ref, act_s, hb_s,
                *, sub, res_w, d_ff, tf, n_sub, skew):
    tm = x_ref.shape[1]
    shift, scale, gate = _mod_rows(mod_ref, sub)

    def row_steps(rows):
        res = {}

        def pre():
            hb_s[rows, :] = _pre_norm(x_ref[0, rows, :], gpre_ref[...], scale, shift).astype(BF16)

        def up(j):
            if j < d_ff // tf:
                hb = hb_s[rows, :]
                res[j] = (_dot(hb, win_ref[:, j * tf:(j + 1) * tf]),
                          _dot(hb, win_ref[:, d_ff + j * tf:d_ff + (j + 1) * tf]))
            if j > 0:
                a, b = res.pop(j - 1)
                act_s[rows, (j - 1) * tf:j * tf] = (_silu(a) * b).astype(BF16)

        def down():
            res["y"] = _dot(act_s[rows, :], wout_ref[...])

        def post():
            o_ref[0, rows, :] = _post_residual(x_ref[0, rows, :], res.pop("y"), gpost_ref[...],
                                               gate, res_w)

        return [pre] + [functools.partial(up, j) for j in range(d_ff // tf + 1)] + [down, post]

    rt = tm // n_sub
    _emit_skewed([row_steps(slice(i * rt, (i + 1) * rt)) for i in range(n_sub)], skew)


def _sublayer_call(body, name, x, mod, layer, params, scratch_shapes, tm, lag=0):
    bsz, s, d = x.shape
    n_tiles = s // tm

    def picked(arr, idx):
        rest = (0,) * (arr.ndim - len(idx))
        return pl.BlockSpec((None,) * len(idx) + arr.shape[len(idx):],
                            lambda b, i: idx + rest, pipeline_mode=pl.Buffered(1))

    return pl.pallas_call(
        body,
        grid=(bsz, n_tiles + lag),
        in_specs=[pl.BlockSpec((1, tm, d), lambda b, i: (b, jnp.minimum(i, n_tiles - 1), 0)),
                  pl.BlockSpec((None, 1, 3 * N_SUB, d), lambda b, i: (layer, b, 0, 0))]
        + [picked(arr, idx) for arr, idx in params],
        out_specs=pl.BlockSpec((1, tm, d), lambda b, i: (b, jnp.maximum(i - lag, 0), 0)),
        out_shape=jax.ShapeDtypeStruct(x.shape, x.dtype),
        scratch_shapes=scratch_shapes,
        compiler_params=pltpu.CompilerParams(
            dimension_semantics=("arbitrary", "arbitrary"),
            vmem_limit_bytes=V7X_VMEM_LIMIT_BYTES),
        name=name,
    )(x, mod, *[arr for arr, _ in params])


def _ffn_sublayer(x, mod, norms, w_in, w_out, *, layer, which, sub, res_w, tm=1024, tf=256,
                  n_sub=4, skew=2):
    d = x.shape[2]
    d_ff = w_out.shape[2]
    body = functools.partial(_ffn_kernel, sub=sub, res_w=res_w, d_ff=d_ff, tf=tf, n_sub=n_sub,
                             skew=skew)
    params = [(norms[0], (layer * N_SUB + sub,)), (norms[1], (layer * N_SUB + sub,)),
              (w_in, (layer, which)), (w_out, (layer, which))]
    return _sublayer_call(body, "ffn_sublayer", x, mod, layer, params,
                          [pltpu.VMEM((tm, d_ff), BF16), pltpu.VMEM((tm, d), BF16)], tm)


def _split3(x):
    hi = x.astype(BF16)
    r1 = x - hi.astype(F32)
    mid = r1.astype(BF16)
    lo = (r1 - mid.astype(F32)).astype(BF16)
    return hi, mid, lo


def _rows_bcast(ref, h, rows, reps):
    pieces = [jnp.broadcast_to(ref[h, pl.ds(r, 1), :], (reps, ref.shape[2])) for r in rows]
    return pieces[0] if len(pieces) == 1 else jnp.concatenate(pieces, axis=0)


def _hg_chunk_heads(q_s, k_s, b_s, v_s, heads, r0, sts, pair_code, sel_ref):
    c = HG_CHUNK
    rows = pl.ds(r0, c)
    hs = range(len(heads))
    q = [q_s[h, rows, :] for h in heads]
    k = [k_s[h, rows, :] for h in heads]
    bc = [b_s[h, rows, :] for h in heads]
    vb = [v_s[h, rows, :] for h in heads]
    ps = [[] for _ in hs]
    for j in range(HG_BLOCK):
        src = [r0 + HG_BLOCK * n + j for n in range(c // HG_BLOCK)]
        for i in hs:
            if j == HG_BLOCK - 1:
                ps[i].append((q[i] * k[i]).astype(BF16))
                continue
            k_j = _rows_bcast(k_s, heads[i], src, HG_BLOCK)
            b_j = _rows_bcast(b_s, heads[i], src, HG_BLOCK)
            ps[i].append((q[i] * k_j * jnp.exp2(jnp.minimum(bc[i] - b_j, 0.0))).astype(BF16))
    stacked = _dot(jnp.concatenate([jnp.concatenate(ps[i], axis=1) for i in hs], axis=0),
                   sel_ref[...])
    scores = [stacked[i * c:(i + 1) * c] for i in hs]
    m = HG_BLOCK
    while m < c:
        mids = [r0 + 2 * m * g + m - 1 for g in range(c // (2 * m))]
        for i in hs:
            b_mid = _rows_bcast(b_s, heads[i], mids, 2 * m)
            dec = jnp.concatenate([(bc[i][r:r + m] - b_mid[r:r + m]) if (r // m) % 2
                                   else (b_mid[r:r + m] - bc[i][r:r + m])
                                   for r in range(0, c, m)], axis=0)
            qk = jnp.concatenate([(q[i] if (r // m) % 2 else k[i])[r:r + m] for r in range(0, c, m)],
                                 axis=0)
            z = (qk * jnp.exp2(dec)).astype(BF16)
            scores[i] = jnp.where(pair_code == m, _dot_nt(z, z), scores[i])
        m *= 2
    st_new = []
    for i in hs:
        b_last = b_s[heads[i], pl.ds(r0 + c - 1, 1), :]
        k_dec = (k[i] * jnp.exp2(b_last - bc[i])).astype(BF16)
        st_new.append(sts[i] * jnp.exp2(b_last) + _dot_tn(vb[i], k_dec))
    o = []
    for i in hs:
        a = jnp.where(pair_code == 0, 0.0, scores[i]).astype(BF16)
        qe = (q[i] * jnp.exp2(bc[i])).astype(BF16)
        o.append(_dot(jnp.concatenate([a, qe], axis=1),
                      jnp.concatenate([vb[i], sts[i].astype(BF16).T], axis=0)))
    return o, st_new


def _hgrn_kernel(x_ref, mod_ref, gpre_ref, gpost_ref, lb_ref, onorm_ref, win_ref, wout_ref,
                 o_ref, q_s, k_s, b_s, g_s, v_s, o_s, y_s, hb_s, st_s, sel_ref,
                 *, sub, res_w, layer, lockstep, proj_heads, n_sub, skew):
    tm = x_ref.shape[1]
    n_heads, _, kd = q_s.shape
    f = n_heads * kd
    c = HG_CHUNK

    @pl.when(pl.program_id(1) == 0)
    def _():
        st_s[...] = jnp.zeros_like(st_s)

    shift, scale, gate = _mod_rows(mod_ref, sub)

    lbr = lb_ref[...]
    e = jnp.exp(lbr - jnp.max(lbr, axis=0, keepdims=True))
    lb = jnp.sum(e[:layer + 1], axis=0, keepdims=True) / jnp.sum(e, axis=0, keepdims=True)

    ri = lax.broadcasted_iota(jnp.int32, (c, c), 0)
    ci = lax.broadcasted_iota(jnp.int32, (c, c), 1)
    tri = (ri >= ci).astype(BF16)

    def projection_steps(r0, nrows):
        rows = slice(r0, r0 + nrows)

        def pre():
            hb_s[rows, :] = _pre_norm(x_ref[0, rows, :], gpre_ref[...], scale, shift).astype(BF16)

        def store_heads(dst, p, val, rr=rows):
            for i in range(proj_heads):
                dst[p * proj_heads + i, rr, :] = val[:, i * kd:(i + 1) * kd]

        def q_piece(p, qv):
            store_heads(q_s, p, _silu(qv))

        def f_piece(p, fx):
            lbp = lb[:, p * proj_heads * kd:(p + 1) * proj_heads * kd]
            ax = jnp.abs(fx)
            en = jnp.exp(-ax)
            inv = 0.5 + 0.5 * jnp.tanh(0.5 * ax)
            pos = fx >= 0.0
            sig_p = jnp.where(pos, inv, en * inv)
            sig_n = jnp.where(pos, en * inv, inv)
            store_heads(k_s, p, (1.0 - lbp) * sig_n)
            log2_f = jnp.log(lbp + (1.0 - lbp) * sig_p) * LOG2_E
            for j in range(nrows // c):
                hi, mid, lo = _split3(log2_f[j * c:(j + 1) * c, :])
                bc = _dot(tri, hi) + _dot(tri, mid) + _dot(tri, lo)
                store_heads(b_s, p, bc, slice(r0 + j * c, r0 + (j + 1) * c))

        def v_piece(p, vv):
            store_heads(v_s, p, vv.astype(BF16))

        def g_piece(p, gv):
            store_heads(g_s, p, _silu(gv))

        work = [(piece, kind, p) for p in range(n_heads // proj_heads)
                for kind, piece in ((1, f_piece), (0, q_piece), (3, g_piece), (2, v_piece))]
        res = {}

        def step(n):
            if n < len(work):
                _, kind, p = work[n]
                lo = kind * f + p * proj_heads * kd
                res[n] = _dot(hb_s[rows, :], win_ref[:, lo:lo + proj_heads * kd])
            if n > 0:
                piece, _, p = work[n - 1]
                piece(p, res.pop(n - 1))

        return [pre] + [functools.partial(step, n) for n in range(len(work) + 1)]

    rt = tm // n_sub
    _emit_skewed([projection_steps(i * rt, rt) for i in range(n_sub)], skew)

    xr = ri ^ ci
    xr = xr | (xr >> 1)
    xr = xr | (xr >> 2)
    xr = xr | (xr >> 4)
    top = xr - (xr >> 1)
    pair_code = jnp.where(ri >= ci, jnp.where(top >= HG_BLOCK, top, 1), 0)
    sel_j = lax.broadcasted_iota(jnp.int32, (HG_BLOCK * kd, c), 0) // kd
    sel_s = lax.broadcasted_iota(jnp.int32, (HG_BLOCK * kd, c), 1) & (HG_BLOCK - 1)
    sel_ref[...] = (sel_j == sel_s).astype(BF16)

    def chunk_body(j, carry):
        r0 = pl.multiple_of(j * c, c)
        for h0 in range(0, n_heads, lockstep):
            heads = list(range(h0, h0 + lockstep))
            o, st_new = _hg_chunk_heads(q_s, k_s, b_s, v_s, heads, r0, [st_s[h] for h in heads],
                                        pair_code, sel_ref)
            for i, h in enumerate(heads):
                o_s[h, pl.ds(r0, c), :] = o[i]
                st_s[h] = st_new[i]
        return carry

    lax.fori_loop(0, tm // c, chunk_body, 0)

    def output_steps(r0, nrows):
        rows = slice(r0, r0 + nrows)
        res = {}

        def head_norm():
            onorm = onorm_ref[...]
            for h in range(n_heads):
                oh = o_s[h, rows, :]
                ms = jnp.mean(oh * oh, axis=-1, keepdims=True)
                y_s[rows, h * kd:(h + 1) * kd] = ((oh * lax.rsqrt(ms + EPS)) * onorm
                                                  * g_s[h, rows, :]).astype(BF16)

        def down():
            res["y"] = _dot(y_s[rows, :], wout_ref[...])

        def post():
            o_ref[0, rows, :] = _post_residual(x_ref[0, rows, :], res.pop("y"), gpost_ref[...],
                                               gate, res_w)

        return [head_norm, down, post]

    _emit_skewed([output_steps(i * rt, rt) for i in range(n_sub)], 1)


def _hgrn_sublayer(x, mod, norms, hg_lb, out_norm, w_in, w_out, *, layer, which, sub, res_w,
                   tm=512, lockstep=8, proj_heads=8, n_sub=1, skew=2):
    d = x.shape[2]
    n_heads = hg_lb.shape[1] // HG_KDIM
    head_buf = pltpu.VMEM((n_heads, tm, HG_KDIM), F32)
    body = functools.partial(_hgrn_kernel, sub=sub, res_w=res_w, layer=layer, lockstep=lockstep,
                             proj_heads=proj_heads, n_sub=n_sub, skew=skew)
    params = [(norms[0], (layer * N_SUB + sub,)), (norms[1], (layer * N_SUB + sub,)),
              (hg_lb, ()), (out_norm, (which,)), (w_in, (which,)), (w_out, (which,))]
    scratch = [head_buf, head_buf, head_buf, head_buf,
               pltpu.VMEM((n_heads, tm, HG_KDIM), BF16),
               head_buf,
               pltpu.VMEM((tm, d), BF16),
               pltpu.VMEM((tm, d), BF16),
               pltpu.VMEM((n_heads, HG_KDIM, HG_KDIM), F32),
               pltpu.VMEM((HG_BLOCK * HG_KDIM, HG_CHUNK), BF16)]
    return _sublayer_call(body, "hgrn_sublayer", x, mod, layer, params, scratch, tm)


GELU_C1 = 0.7978845608028654
GELU_C3 = GELU_C1 * 0.044715


def _gelu_tanh_x2(x):
    return x + x * jnp.tanh(x * (GELU_C1 + GELU_C3 * (x * x)))


def _gmlp_kernel(x_ref, mod_ref, gpre_ref, gpost_ref, win_ref, bin_ref, lng_ref, lnb_ref,
                 ws_ref, bs_ref, wout_ref, o_ref, z_s, vn_s, act_s, hb_s,
                 *, sub, res_w, tn, n_sub, skew):
    tm = x_ref.shape[1]
    half = vn_s.shape[1]
    n_groups = ws_ref.shape[0]
    gdim = half // n_groups
    n_tiles = half // tn
    c = GM_CHUNK

    shift, scale, gate = _mod_rows(mod_ref, sub)
    ri = lax.broadcasted_iota(jnp.int32, (c, c), 0)
    ci = lax.broadcasted_iota(jnp.int32, (c, c), 1)
    tril = ri >= ci
    n_down = 4
    down_tn = wout_ref.shape[1] // n_down

    def row_steps(r0, nrows):
        rows = slice(r0, r0 + nrows)
        res, sums, sqs = {}, [], []

        def pre():
            hb_s[rows, :] = _pre_norm(x_ref[0, rows, :], gpre_ref[...], scale, shift).astype(BF16)

        def project(idx):
            lo = (half + idx * tn) if idx < n_tiles else (idx - n_tiles) * tn
            res[idx] = _dot(hb_s[rows, :], win_ref[:, lo:lo + tn])

        def v_act(j):
            z = _gelu_tanh_x2(res.pop(j) + bin_ref[:, half + j * tn:half + (j + 1) * tn])
            z_s[rows, j * tn:(j + 1) * tn] = z
            sums.append(jnp.sum(z, axis=-1, keepdims=True))
            sqs.append(jnp.sum(z * z, axis=-1, keepdims=True))

        def normalize():
            mu = sum(sums) * (1.0 / half)
            var = sum(sqs) * (1.0 / half) - mu * mu
            rstd = lax.rsqrt(var + 4.0 * EPS)
            for j in range(n_tiles):
                sl = slice(j * tn, (j + 1) * tn)
                vn_s[rows, sl] = (((z_s[rows, sl] - mu) * rstd) * lng_ref[:, sl]
                                  + lnb_ref[:, sl]).astype(BF16)

        def u_act(j):
            u = _gelu_tanh_x2(res.pop(n_tiles + j) + bin_ref[:, j * tn:(j + 1) * tn])
            for g in range(j * tn // gdim, ((j + 1) * tn - 1) // gdim + 1):
                lo, hi = max(g * gdim, j * tn), min((g + 1) * gdim, (j + 1) * tn)
                wsg = jnp.where(tril, 0.5 * ws_ref[g], 0.0).astype(BF16)
                bsg = 0.5 * bs_ref[g]
                for i in range(nrows // c):
                    rc = slice(r0 + i * c, r0 + (i + 1) * c)
                    vm = _dot(wsg, vn_s[rc, lo:hi]) + bsg
                    ug = u[i * c:(i + 1) * c, lo - j * tn:hi - j * tn]
                    act_s[rc, lo:hi] = (ug * vm).astype(BF16)

        def step(idx):
            if idx < 2 * n_tiles:
                project(idx)
            if 1 <= idx <= n_tiles:
                v_act(idx - 1)
            if idx == n_tiles + 1:
                normalize()
            if idx > n_tiles:
                u_act(idx - n_tiles - 1)

        def down(n):
            res["y", n] = _dot(act_s[rows, :], wout_ref[:, n * down_tn:(n + 1) * down_tn])

        def post():
            y = jnp.concatenate([res.pop(("y", n)) for n in range(n_down)], axis=1)
            o_ref[0, rows, :] = _post_residual(x_ref[0, rows, :], y, gpost_ref[...], gate, res_w)

        return ([pre] + [functools.partial(step, idx) for idx in range(2 * n_tiles + 1)]
                + [functools.partial(down, n) for n in range(n_down)] + [post])

    rt = tm // n_sub
    _emit_skewed([row_steps(i * rt, rt) for i in range(n_sub)], skew)


def _gmlp_sublayer(x, mod, norms, w_in, b_in, ln_g, ln_b, w_s, b_s, w_out, *, layer, which, sub,
                   res_w, tm=512, tn=256, n_sub=1, skew=2):
    d = x.shape[2]
    half = w_out.shape[1]
    body = functools.partial(_gmlp_kernel, sub=sub, res_w=res_w, tn=tn, n_sub=n_sub, skew=skew)
    params = [(norms[0], (layer * N_SUB + sub,)), (norms[1], (layer * N_SUB + sub,)),
              (w_in, (which,)), (b_in, (which,)), (ln_g, (which,)), (ln_b, (which,)),
              (w_s, (which,)), (b_s, (which,)), (w_out, (which,))]
    scratch = [pltpu.VMEM((tm, half), F32), pltpu.VMEM((tm, half), BF16),
               pltpu.VMEM((tm, half), BF16), pltpu.VMEM((tm, d), BF16)]
    return _sublayer_call(body, "gmlp_sublayer", x, mod, layer, params, scratch, tm)


def kernel(x, c, ada_w, ada_b, norm_pre, norm_post, ffn_w_in, ffn_w_out, hg_w_in, hg_w_out,
           hg_out_norm, hg_lb, gm_w_in, gm_b_in, gm_ln_g, gm_ln_b, gm_w_s, gm_b_s, gm_w_out):
    depth, d = ada_w.shape[0], x.shape[2]
    mod = _modulation(c, ada_w, ada_b)
    norms = (norm_pre.reshape(depth * N_SUB, 1, d), norm_post.reshape(depth * N_SUB, 1, d))
    hg_out_norm = hg_out_norm[:, None, :]
    gm_b_in, gm_ln_g, gm_ln_b = gm_b_in[:, None, :], gm_ln_g[:, None, :], gm_ln_b[:, None, :]
    gm_b_s = gm_b_s[..., None]
    ffn_w_in = ffn_w_in.astype(BF16)
    ffn_w_out = ffn_w_out.astype(BF16)
    hg_w_in = hg_w_in.astype(BF16)
    hg_w_out = hg_w_out.astype(BF16)
    gm_w_in = gm_w_in.astype(BF16)
    gm_w_out = gm_w_out.astype(BF16)
    for i in range(depth):
        j = i // 2
        x = _ffn_sublayer(x, mod, norms, ffn_w_in, ffn_w_out, layer=i, which=0, sub=0, res_w=0.5)
        if i % 2 == 0:
            x = _hgrn_sublayer(x, mod, norms, hg_lb, hg_out_norm, hg_w_in, hg_w_out, layer=i,
                               which=j, sub=1, res_w=1.0)
        else:
            x = _gmlp_sublayer(x, mod, norms, gm_w_in, gm_b_in, gm_ln_g, gm_ln_b, gm_w_s, gm_b_s,
                               gm_w_out, layer=i, which=j, sub=1, res_w=1.0)
        x = _ffn_sublayer(x, mod, norms, ffn_w_in, ffn_w_out, layer=i, which=1, sub=2, res_w=0.5)
    return x
```

```python
import functools

import jax
import jax.numpy as jnp
from jax import lax
from jax.experimental import pallas as pl
from jax.experimental.pallas import tpu as pltpu

EPS = 1e-6
LOG2_E = 1.4426950408889634
N_SUB = 3
HG_KDIM = 128
HG_CHUNK = 128
HG_BLOCK = 8
GM_CHUNK = 128
GM_GROUPS = 8

V7X_VMEM_LIMIT_BYTES = 56 * 1024 * 1024

F32 = jnp.float32
BF16 = jnp.bfloat16


def _dot(a, b):
    return jnp.dot(a, b, preferred_element_type=F32)


def _dot_nt(a, b):
    return lax.dot_general(a, b, (((1,), (1,)), ((), ())), preferred_element_type=F32)


def _dot_tn(a, b):
    return lax.dot_general(a, b, (((0,), (0,)), ((), ())), preferred_element_type=F32)


def _silu(x):
    hx = 0.5 * x
    return hx + hx * jnp.tanh(hx)


def _pre_norm(x, g, scale, shift):
    ms = jnp.mean(x * x, axis=-1, keepdims=True)
    return (x * lax.rsqrt(ms + EPS)) * (g * (1.0 + scale)) + shift


def _post_residual(x, y, g, gate, res_w):
    ms = jnp.mean(y * y, axis=-1, keepdims=True)
    return x + (y * lax.rsqrt(ms + EPS)) * ((res_w * gate) * g)


def _mod_rows(mod_ref, sub):
    shift = mod_ref[0, 3 * sub:3 * sub + 1, :]
    scale = mod_ref[0, 3 * sub + 1:3 * sub + 2, :]
    gate = mod_ref[0, 3 * sub + 2:3 * sub + 3, :]
    return shift, scale, gate


def _mod_kernel(ct_ref, w_ref, b_ref, o_ref, *, n_batch):
    ct = ct_ref[...]
    cond = _silu(ct)
    w = w_ref[0]
    rows = [jnp.sum(w * cond[:, b:b + 1], axis=0, keepdims=True) + b_ref[0]
            for b in range(n_batch)]
    pad = jnp.zeros((o_ref.shape[1] - n_batch, w.shape[1]), F32)
    o_ref[0] = jnp.concatenate(rows + [pad], axis=0)


def _modulation(c, ada_w, ada_b):
    depth, d, n = ada_w.shape
    b = c.shape[0]
    rows = 8
    tn = 1024
    out = pl.pallas_call(
        functools.partial(_mod_kernel, n_batch=b),
        grid=(depth, n // tn),
        in_specs=[
            pl.BlockSpec((d, b), lambda i, j: (0, 0)),
            pl.BlockSpec((1, d, tn), lambda i, j: (i, 0, j)),
            pl.BlockSpec((1, 1, tn), lambda i, j: (i, 0, j)),
        ],
        out_specs=pl.BlockSpec((1, rows, tn), lambda i, j: (i, 0, j)),
        out_shape=jax.ShapeDtypeStruct((depth, rows, n), F32),
        compiler_params=pltpu.CompilerParams(
            dimension_semantics=("arbitrary", "arbitrary")),
        name="adaln_mod",
    )(c.T, ada_w, ada_b.reshape(depth, 1, n))
    return out.reshape(depth, rows, 3 * N_SUB, d)


def _emit_skewed(streams, skew):
    n = max(len(s) + i * skew for i, s in enumerate(streams))
    for k in range(n):
        for i, s in enumerate(streams):
            if 0 <= k - i * skew < len(s):
                s[k - i * skew]()


def _ffn_kernel(x_ref, mod_ref, gpre_ref, gpost_ref, win_ref, wout_ref, o_ref, act_s, hb_s,
                *, sub, res_w, d_ff, tf, n_sub, skew):
    tm = x_ref.shape[1]
    shift, scale, gate = _mod_rows(mod_ref, sub)

    def row_steps(rows):
        res = {}

        def pre():
            hb_s[rows, :] = _pre_norm(x_ref[0, rows, :], gpre_ref[...], scale, shift).astype(BF16)

        def up(j):
            if j < d_ff // tf:
                hb = hb_s[rows, :]
                res[j] = (_dot(hb, win_ref[:, j * tf:(j + 1) * tf]),
                          _dot(hb, win_ref[:, d_ff + j * tf:d_ff + (j + 1) * tf]))
            if j > 0:
                a, b = res.pop(j - 1)
                act_s[rows, (j - 1) * tf:j * tf] = (_silu(a) * b).astype(BF16)

        def down():
            res["y"] = _dot(act_s[rows, :], wout_ref[...])

        def post():
            o_ref[0, rows, :] = _post_residual(x_ref[0, rows, :], res.pop("y"), gpost_ref[...],
                                               gate, res_w)

        return [pre] + [functools.partial(up, j) for j in range(d_ff // tf + 1)] + [down, post]

    rt = tm // n_sub
    _emit_skewed([row_steps(slice(i * rt, (i + 1) * rt)) for i in range(n_sub)], skew)


def _sublayer_call(body, name, x, mod, layer, params, scratch_shapes, tm, lag=0):
    bsz, s, d = x.shape
    n_tiles = s // tm

    def picked(arr, idx):
        rest = (0,) * (arr.ndim - len(idx))
        return pl.BlockSpec((None,) * len(idx) + arr.shape[len(idx):],
                            lambda b, i: idx + rest, pipeline_mode=pl.Buffered(1))

    return pl.pallas_call(
        body,
        grid=(bsz, n_tiles + lag),
        in_specs=[pl.BlockSpec((1, tm, d), lambda b, i: (b, jnp.minimum(i, n_tiles - 1), 0)),
                  pl.BlockSpec((None, 1, 3 * N_SUB, d), lambda b, i: (layer, b, 0, 0))]
        + [picked(arr, idx) for arr, idx in params],
        out_specs=pl.BlockSpec((1, tm, d), lambda b, i: (b, jnp.maximum(i - lag, 0), 0)),
        out_shape=jax.ShapeDtypeStruct(x.shape, x.dtype),
        scratch_shapes=scratch_shapes,
        compiler_params=pltpu.CompilerParams(
            dimension_semantics=("arbitrary", "arbitrary"),
            vmem_limit_bytes=V7X_VMEM_LIMIT_BYTES),
        name=name,
    )(x, mod, *[arr for arr, _ in params])


def _ffn_sublayer(x, mod, norms, w_in, w_out, *, layer, which, sub, res_w, tm=1024, tf=256,
                  n_sub=4, skew=2):
    d = x.shape[2]
    d_ff = w_out.shape[2]
    body = functools.partial(_ffn_kernel, sub=sub, res_w=res_w, d_ff=d_ff, tf=tf, n_sub=n_sub,
                             skew=skew)
    params = [(norms[0], (layer * N_SUB + sub,)), (norms[1], (layer * N_SUB + sub,)),
              (w_in, (layer, which)), (w_out, (layer, which))]
    return _sublayer_call(body, "ffn_sublayer", x, mod, layer, params,
                          [pltpu.VMEM((tm, d_ff), BF16), pltpu.VMEM((tm, d), BF16)], tm)


def _split3(x):
    hi = x.astype(BF16)
    r1 = x - hi.astype(F32)
    mid = r1.astype(BF16)
    lo = (r1 - mid.astype(F32)).astype(BF16)
    return hi, mid, lo


def _rows_bcast(ref, h, rows, reps):
    pieces = [jnp.broadcast_to(ref[h, pl.ds(r, 1), :], (reps, ref.shape[2])) for r in rows]
    return pieces[0] if len(pieces) == 1 else jnp.concatenate(pieces, axis=0)


def _hg_chunk_heads(q_s, k_s, b_s, v_s, heads, r0, sts, pair_code, sel_ref):
    c = HG_CHUNK
    rows = pl.ds(r0, c)
    hs = range(len(heads))
    q = [q_s[h, rows, :] for h in heads]
    k = [k_s[h, rows, :] for h in heads]
    bc = [b_s[h, rows, :] for h in heads]
    vb = [v_s[h, rows, :] for h in heads]
    ps = [[] for _ in hs]
    for j in range(HG_BLOCK):
        src = [r0 + HG_BLOCK * n + j for n in range(c // HG_BLOCK)]
        for i in hs:
            if j == HG_BLOCK - 1:
                ps[i].append((q[i] * k[i]).astype(BF16))
                continue
            k_j = _rows_bcast(k_s, heads[i], src, HG_BLOCK)
            b_j = _rows_bcast(b_s, heads[i], src, HG_BLOCK)
            ps[i].append((q[i] * k_j * jnp.exp2(jnp.minimum(bc[i] - b_j, 0.0))).astype(BF16))
    stacked = _dot(jnp.concatenate([jnp.concatenate(ps[i], axis=1) for i in hs], axis=0),
                   sel_ref[...])
    scores = [stacked[i * c:(i + 1) * c] for i in hs]
    m = HG_BLOCK
    while m < c:
        mids = [r0 + 2 * m * g + m - 1 for g in range(c // (2 * m))]
        for i in hs:
            b_mid = _rows_bcast(b_s, heads[i], mids, 2 * m)
            dec = jnp.concatenate([(bc[i][r:r + m] - b_mid[r:r + m]) if (r // m) % 2
                                   else (b_mid[r:r + m] - bc[i][r:r + m])
                                   for r in range(0, c, m)], axis=0)
            qk = jnp.concatenate([(q[i] if (r // m) % 2 else k[i])[r:r + m] for r in range(0, c, m)],
                                 axis=0)
            z = (qk * jnp.exp2(dec)).astype(BF16)
            scores[i] = jnp.where(pair_code == m, _dot_nt(z, z), scores[i])
        m *= 2
    st_new = []
    for i in hs:
        b_last = b_s[heads[i], pl.ds(r0 + c - 1, 1), :]
        k_dec = (k[i] * jnp.exp2(b_last - bc[i])).astype(BF16)
        st_new.append(sts[i] * jnp.exp2(b_last) + _dot_tn(vb[i], k_dec))
    o = []
    for i in hs:
        a = jnp.where(pair_code == 0, 0.0, scores[i]).astype(BF16)
        qe = (q[i] * jnp.exp2(bc[i])).astype(BF16)
        o.append(_dot(jnp.concatenate([a, qe], axis=1),
                      jnp.concatenate([vb[i], sts[i].astype(BF16).T], axis=0)))
    return o, st_new


def _hgrn_kernel(x_ref, mod_ref, gpre_ref, gpost_ref, lb_ref, onorm_ref, win_ref, wout_ref,
                 o_ref, q_s, k_s, b_s, g_s, v_s, o_s, y_s, hb_s, st_s, sel_ref,
                 *, sub, res_w, layer, lockstep, proj_heads, n_sub, skew):
    tm = x_ref.shape[1]
    n_heads, _, kd = q_s.shape
    f = n_heads * kd
    c = HG_CHUNK

    @pl.when(pl.program_id(1) == 0)
    def _():
        st_s[...] = jnp.zeros_like(st_s)

    shift, scale, gate = _mod_rows(mod_ref, sub)

    lbr = lb_ref[...]
    e = jnp.exp(lbr - jnp.max(lbr, axis=0, keepdims=True))
    lb = jnp.sum(e[:layer + 1], axis=0, keepdims=True) / jnp.sum(e, axis=0, keepdims=True)

    ri = lax.broadcasted_iota(jnp.int32, (c, c), 0)
    ci = lax.broadcasted_iota(jnp.int32, (c, c), 1)
    tri = (ri >= ci).astype(BF16)

    def projection_steps(r0, nrows):
        rows = slice(r0, r0 + nrows)

        def pre():
            hb_s[rows, :] = _pre_norm(x_ref[0, rows, :], gpre_ref[...], scale, shift).astype(BF16)

        def store_heads(dst, p, val, rr=rows):
            for i in range(proj_heads):
                dst[p * proj_heads + i, rr, :] = val[:, i * kd:(i + 1) * kd]

        def q_piece(p, qv):
            store_heads(q_s, p, _silu(qv))

        def f_piece(p, fx):
            lbp = lb[:, p * proj_heads * kd:(p + 1) * proj_heads * kd]
            ax = jnp.abs(fx)
            en = jnp.exp(-ax)
            inv = 0.5 + 0.5 * jnp.tanh(0.5 * ax)
            pos = fx >= 0.0
            sig_p = jnp.where(pos, inv, en * inv)
            sig_n = jnp.where(pos, en * inv, inv)
            store_heads(k_s, p, (1.0 - lbp) * sig_n)
            log2_f = jnp.log(lbp + (1.0 - lbp) * sig_p) * LOG2_E
            for j in range(nrows // c):
                hi, mid, lo = _split3(log2_f[j * c:(j + 1) * c, :])
                bc = _dot(tri, hi) + _dot(tri, mid) + _dot(tri, lo)
                store_heads(b_s, p, bc, slice(r0 + j * c, r0 + (j + 1) * c))

        def v_piece(p, vv):
            store_heads(v_s, p, vv.astype(BF16))

        def g_piece(p, gv):
            store_heads(g_s, p, _silu(gv))

        work = [(piece, kind, p) for p in range(n_heads // proj_heads)
                for kind, piece in ((1, f_piece), (0, q_piece), (3, g_piece), (2, v_piece))]
        res = {}

        def step(n):
            if n < len(work):
                _, kind, p = work[n]
                lo = kind * f + p * proj_heads * kd
                res[n] = _dot(hb_s[rows, :], win_ref[:, lo:lo + proj_heads * kd])
            if n > 0:
                piece, _, p = work[n - 1]
                piece(p, res.pop(n - 1))

        return [pre] + [functools.partial(step, n) for n in range(len(work) + 1)]

    rt = tm // n_sub
    _emit_skewed([projection_steps(i * rt, rt) for i in range(n_sub)], skew)

    xr = ri ^ ci
    xr = xr | (xr >> 1)
    xr = xr | (xr >> 2)
    xr = xr | (xr >> 4)
    top = xr - (xr >> 1)
    pair_code = jnp.where(ri >= ci, jnp.where(top >= HG_BLOCK, top, 1), 0)
    sel_j = lax.broadcasted_iota(jnp.int32, (HG_BLOCK * kd, c), 0) // kd
    sel_s = lax.broadcasted_iota(jnp.int32, (HG_BLOCK * kd, c), 1) & (HG_BLOCK - 1)
    sel_ref[...] = (sel_j == sel_s).astype(BF16)

    def chunk_body(j, carry):
        r0 = pl.multiple_of(j * c, c)
        for h0 in range(0, n_heads, lockstep):
            heads = list(range(h0, h0 + lockstep))
            o, st_new = _hg_chunk_heads(q_s, k_s, b_s, v_s, heads, r0, [st_s[h] for h in heads],
                                        pair_code, sel_ref)
            for i, h in enumerate(heads):
                o_s[h, pl.ds(r0, c), :] = o[i]
                st_s[h] = st_new[i]
        return carry

    lax.fori_loop(0, tm // c, chunk_body, 0, unroll=2)

    def output_steps(r0, nrows):
        rows = slice(r0, r0 + nrows)
        res = {}

        def head_norm():
            onorm = onorm_ref[...]
            for h in range(n_heads):
                oh = o_s[h, rows, :]
                ms = jnp.mean(oh * oh, axis=-1, keepdims=True)
                y_s[rows, h * kd:(h + 1) * kd] = ((oh * lax.rsqrt(ms + EPS)) * onorm
                                                  * g_s[h, rows, :]).astype(BF16)

        def down():
            res["y"] = _dot(y_s[rows, :], wout_ref[...])

        def post():
            o_ref[0, rows, :] = _post_residual(x_ref[0, rows, :], res.pop("y"), gpost_ref[...],
                                               gate, res_w)

        return [head_norm, down, post]

    _emit_skewed([output_steps(i * rt, rt) for i in range(n_sub)], 1)


def _hgrn_sublayer(x, mod, norms, hg_lb, out_norm, w_in, w_out, *, layer, which, sub, res_w,
                   tm=512, lockstep=8, proj_heads=8, n_sub=1, skew=2):
    d = x.shape[2]
    n_heads = hg_lb.shape[1] // HG_KDIM
    head_buf = pltpu.VMEM((n_heads, tm, HG_KDIM), F32)
    body = functools.partial(_hgrn_kernel, sub=sub, res_w=res_w, layer=layer, lockstep=lockstep,
                             proj_heads=proj_heads, n_sub=n_sub, skew=skew)
    params = [(norms[0], (layer * N_SUB + sub,)), (norms[1], (layer * N_SUB + sub,)),
              (hg_lb, ()), (out_norm, (which,)), (w_in, (which,)), (w_out, (which,))]
    scratch = [head_buf, head_buf, head_buf, head_buf,
               pltpu.VMEM((n_heads, tm, HG_KDIM), BF16),
               head_buf,
               pltpu.VMEM((tm, d), BF16),
               pltpu.VMEM((tm, d), BF16),
               pltpu.VMEM((n_heads, HG_KDIM, HG_KDIM), F32),
               pltpu.VMEM((HG_BLOCK * HG_KDIM, HG_CHUNK), BF16)]
    return _sublayer_call(body, "hgrn_sublayer", x, mod, layer, params, scratch, tm)


GELU_C1 = 0.7978845608028654
GELU_C3 = GELU_C1 * 0.044715


def _gelu_tanh_x2(x):
    return x + x * jnp.tanh(x * (GELU_C1 + GELU_C3 * (x * x)))


def _gmlp_kernel(x_ref, mod_ref, gpre_ref, gpost_ref, win_ref, bin_ref, lng_ref, lnb_ref,
                 ws_ref, bs_ref, wout_ref, o_ref, z_s, vn_s, act_s, hb_s,
                 *, sub, res_w, tn, n_sub, skew):
    tm = x_ref.shape[1]
    half = vn_s.shape[1]
    n_groups = ws_ref.shape[0]
    gdim = half // n_groups
    n_tiles = half // tn
    c = GM_CHUNK

    shift, scale, gate = _mod_rows(mod_ref, sub)
    ri = lax.broadcasted_iota(jnp.int32, (c, c), 0)
    ci = lax.broadcasted_iota(jnp.int32, (c, c), 1)
    tril = ri >= ci
    n_down = 4
    down_tn = wout_ref.shape[1] // n_down

    def row_steps(r0, nrows):
        rows = slice(r0, r0 + nrows)
        res, sums, sqs = {}, [], []

        def pre():
            hb_s[rows, :] = _pre_norm(x_ref[0, rows, :], gpre_ref[...], scale, shift).astype(BF16)

        def project(idx):
            lo = (half + idx * tn) if idx < n_tiles else (idx - n_tiles) * tn
            res[idx] = _dot(hb_s[rows, :], win_ref[:, lo:lo + tn])

        def v_act(j):
            z = _gelu_tanh_x2(res.pop(j) + bin_ref[:, half + j * tn:half + (j + 1) * tn])
            z_s[rows, j * tn:(j + 1) * tn] = z
            sums.append(jnp.sum(z, axis=-1, keepdims=True))
            sqs.append(jnp.sum(z * z, axis=-1, keepdims=True))

        def normalize():
            mu = sum(sums) * (1.0 / half)
            var = sum(sqs) * (1.0 / half) - mu * mu
            rstd = lax.rsqrt(var + 4.0 * EPS)
            for j in range(n_tiles):
                sl = slice(j * tn, (j + 1) * tn)
                vn_s[rows, sl] = (((z_s[rows, sl] - mu) * rstd) * lng_ref[:, sl]
                                  + lnb_ref[:, sl]).astype(BF16)

        def u_act(j):
            u = _gelu_tanh_x2(res.pop(n_tiles + j) + bin_ref[:, j * tn:(j + 1) * tn])
            for g in range(j * tn // gdim, ((j + 1) * tn - 1) // gdim + 1):
                lo, hi = max(g * gdim, j * tn), min((g + 1) * gdim, (j + 1) * tn)
                wsg = jnp.where(tril, 0.5 * ws_ref[g], 0.0).astype(BF16)
                bsg = 0.5 * bs_ref[g]
                for i in range(nrows // c):
                    rc = slice(r0 + i * c, r0 + (i + 1) * c)
                    vm = _dot(wsg, vn_s[rc, lo:hi]) + bsg
                    ug = u[i * c:(i + 1) * c, lo - j * tn:hi - j * tn]
                    act_s[rc, lo:hi] = (ug * vm).astype(BF16)

        def step(idx):
            if idx < 2 * n_tiles:
                project(idx)
            if 1 <= idx <= n_tiles:
                v_act(idx - 1)
            if idx == n_tiles + 1:
                normalize()
            if idx > n_tiles:
                u_act(idx - n_tiles - 1)

        def down(n):
            res["y", n] = _dot(act_s[rows, :], wout_ref[:, n * down_tn:(n + 1) * down_tn])

        def post():
            y = jnp.concatenate([res.pop(("y", n)) for n in range(n_down)], axis=1)
            o_ref[0, rows, :] = _post_residual(x_ref[0, rows, :], y, gpost_ref[...], gate, res_w)

        return ([pre] + [functools.partial(step, idx) for idx in range(2 * n_tiles + 1)]
                + [functools.partial(down, n) for n in range(n_down)] + [post])

    rt = tm // n_sub
    _emit_skewed([row_steps(i * rt, rt) for i in range(n_sub)], skew)


def _gmlp_sublayer(x, mod, norms, w_in, b_in, ln_g, ln_b, w_s, b_s, w_out, *, layer, which, sub,
                   res_w, tm=512, tn=256, n_sub=1, skew=2):
    d = x.shape[2]
    half = w_out.shape[1]
    body = functools.partial(_gmlp_kernel, sub=sub, res_w=res_w, tn=tn, n_sub=n_sub, skew=skew)
    params = [(norms[0], (layer * N_SUB + sub,)), (norms[1], (layer * N_SUB + sub,)),
              (w_in, (which,)), (b_in, (which,)), (ln_g, (which,)), (ln_b, (which,)),
              (w_s, (which,)), (b_s, (which,)), (w_out, (which,))]
    scratch = [pltpu.VMEM((tm, half), F32), pltpu.VMEM((tm, half), BF16),
               pltpu.VMEM((tm, half), BF16), pltpu.VMEM((tm, d), BF16)]
    return _sublayer_call(body, "gmlp_sublayer", x, mod, layer, params, scratch, tm)


def kernel(x, c, ada_w, ada_b, norm_pre, norm_post, ffn_w_in, ffn_w_out, hg_w_in, hg_w_out,
           hg_out_norm, hg_lb, gm_w_in, gm_b_in, gm_ln_g, gm_ln_b, gm_w_s, gm_b_s, gm_w_out):
    depth, d = ada_w.shape[0], x.shape[2]
    mod = _modulation(c, ada_w, ada_b)
    norms = (norm_pre.reshape(depth * N_SUB, 1, d), norm_post.reshape(depth * N_SUB, 1, d))
    hg_out_norm = hg_out_norm[:, None, :]
    gm_b_in, gm_ln_g, gm_ln_b = gm_b_in[:, None, :], gm_ln_g[:, None, :], gm_ln_b[:, None, :]
    gm_b_s = gm_b_s[..., None]
    ffn_w_in = ffn_w_in.astype(BF16)
    ffn_w_out = ffn_w_out.astype(BF16)
    hg_w_in = hg_w_in.astype(BF16)
    hg_w_out = hg_w_out.astype(BF16)
    gm_w_in = gm_w_in.astype(BF16)
    gm_w_out = gm_w_out.astype(BF16)
    for i in range(depth):
        j = i // 2
        x = _ffn_sublayer(x, mod, norms, ffn_w_in, ffn_w_out, layer=i, which=0, sub=0, res_w=0.5)
        if i % 2 == 0:
            x = _hgrn_sublayer(x, mod, norms, hg_lb, hg_out_norm, hg_w_in, hg_w_out, layer=i,
                               which=j, sub=1, res_w=1.0)
        else:
            x = _gmlp_sublayer(x, mod, norms, gm_w_in, gm_b_in, gm_ln_g, gm_ln_b, gm_w_s, gm_b_s,
                               gm_w_out, layer=i, which=j, sub=1, res_w=1.0)
        x = _ffn_sublayer(x, mod, norms, ffn_w_in, ffn_w_out, layer=i, which=1, sub=2, res_w=0.5)
    return x
```
